```python
import math
import jax, jax.numpy as jnp
from jax import lax
import numpy as np

D_MODEL = 1024
BATCH = 4
SEQ = 8192
DEPTH = 2

N_ATTN_LAYERS = (DEPTH + 1) // 2
N_REC_LAYERS = DEPTH // 2
HEAD_DIM = 64
BLOCK = 128
A_Q_HEADS = 8
A_KV_HEADS = 2
A_WINDOW = 128
B_HEADS = 8
B_BRANCHES = ((128, 1), (512, 4), (2048, 16))
N_ATTN_HEADS = A_Q_HEADS + B_HEADS
ATTN_SPLITS = [A_Q_HEADS * HEAD_DIM, A_KV_HEADS * HEAD_DIM, A_KV_HEADS * HEAD_DIM,
               B_HEADS * HEAD_DIM, B_HEADS * HEAD_DIM, B_HEADS * HEAD_DIM]
ATTN_IN = sum(ATTN_SPLITS)
ATTN_OUT = N_ATTN_HEADS * HEAD_DIM
S5_GROUP = 16
S5_GROUPS = 16
S5_WIDTH = S5_GROUP * S5_GROUPS
S5_STATE = 64
DN_HEADS = 6
DN_DK = 128
DN_DV = 128
DN_CONV = 4
DN_CHUNK = 64
REC_SPLITS = [S5_WIDTH, DN_HEADS * DN_DK, DN_HEADS * DN_DK, DN_HEADS * DN_DV,
              DN_HEADS * DN_DV, DN_HEADS, DN_HEADS]
REC_IN = sum(REC_SPLITS)
REC_OUT = S5_WIDTH + DN_HEADS * DN_DV
D_FF = 2816
FFN_CONV = 3
EPS = 1e-6

kernel_name = "hybrid_swa_dilated_s5_deltanet_block"


def split_cols(t, sizes):
    offs = np.cumsum(sizes)[:-1]
    return jnp.split(t, [int(o) for o in offs], axis=-1)


def rms_norm(x, w):
    xf = x.astype(jnp.float32)
    y = xf * lax.rsqrt(jnp.mean(xf * xf, axis=-1, keepdims=True) + EPS)
    return (y * w.astype(jnp.float32)).astype(x.dtype)


def causal_dwconv(x, w):
    width, ch = w.shape
    xp = jnp.pad(x, ((0, 0), (width - 1, 0), (0, 0)))
    return lax.conv_general_dilated(xp, w[:, None, :].astype(x.dtype), window_strides=(1,),
                                    padding="VALID", dimension_numbers=("NWC", "WIO", "NWC"),
                                    feature_group_count=ch)


def alibi_slopes(n):
    return jnp.asarray(2.0 ** (-8.0 * np.arange(1, n + 1) / n), dtype=jnp.float32)


def banded_attention(q, k, v, slopes, step, max_dist):
    b, L, K, R, hd = q.shape
    nb = L // BLOCK
    qb = q.reshape(b, nb, BLOCK, K, R, hd)
    pad = ((0, 0), (BLOCK, 0), (0, 0), (0, 0))
    kb = jnp.pad(k, pad).reshape(b, nb + 1, BLOCK, K, hd)
    vb = jnp.pad(v, pad).reshape(b, nb + 1, BLOCK, K, hd)
    kw = jnp.concatenate([kb[:, :-1], kb[:, 1:]], axis=2)
    vw = jnp.concatenate([vb[:, :-1], vb[:, 1:]], axis=2)
    s = jnp.einsum("bnqkrd,bnskd->bnkrqs", qb, kw,
                   preferred_element_type=jnp.float32) * (hd ** -0.5)
    dist = BLOCK + jnp.arange(BLOCK)[:, None] - jnp.arange(2 * BLOCK)[None, :]
    after_start = (jnp.arange(nb)[:, None, None] > 0) | (jnp.arange(2 * BLOCK)[None, None, :] >= BLOCK)
    valid = (dist >= 0) & (dist <= max_dist) & after_start
    bias = -slopes.astype(jnp.float32)[:, :, None, None] * (step * dist).astype(jnp.float32)
    s = jnp.where(valid[None, :, None, None], s + bias, -jnp.inf)
    m = jnp.max(s, axis=-1, keepdims=True)
    p = jnp.exp(s - m)
    l = jnp.sum(p, axis=-1)
    o = jnp.einsum("bnkrqs,bnskd->bnqkrd", p.astype(v.dtype), vw,
                   preferred_element_type=jnp.float32)
    o = o / jnp.moveaxis(l, -1, 2)[..., None]
    lse = jnp.moveaxis(m[..., 0] + jnp.log(l), -1, 2)
    return o.reshape(b, L, K, R, hd).astype(q.dtype), lse.reshape(b, L, K, R)


def dilated_branch(q, k, v, slopes, window, dilation):
    b, L, H, hd = q.shape
    span = dilation * BLOCK
    Lp = -(-L // span) * span
    pad = ((0, 0), (0, Lp - L), (0, 0), (0, 0))

    def strided(t):
        return jnp.pad(t, pad).reshape(b, Lp // dilation, dilation * H, hd)

    o, lse = banded_attention(strided(q)[:, :, :, None, :], strided(k), strided(v),
                              jnp.tile(slopes, dilation)[:, None], dilation, window // dilation)
    o = o[:, :, :, 0].reshape(b, Lp, H, hd)[:, :L]
    lse = lse[..., 0].reshape(b, Lp, H)[:, :L]
    return o, lse


def attention_mixer(h, w_in, q_norm_a, k_norm_a, q_norm_b, k_norm_b, sinks, w_out):
    b, L, _ = h.shape
    rep = A_Q_HEADS // A_KV_HEADS
    qa, ka, va, qb, kb, vb = split_cols(h @ w_in, ATTN_SPLITS)
    slopes = alibi_slopes(N_ATTN_HEADS)
    qa = rms_norm(qa.reshape(b, L, A_KV_HEADS, rep, HEAD_DIM), q_norm_a)
    ka = rms_norm(ka.reshape(b, L, A_KV_HEADS, HEAD_DIM), k_norm_a)
    va = va.reshape(b, L, A_KV_HEADS, HEAD_DIM)
    oa, lse_a = banded_attention(qa, ka, va, slopes[:A_Q_HEADS].reshape(A_KV_HEADS, rep),
                                 1, A_WINDOW - 1)
    keep = jax.nn.sigmoid(lse_a - sinks.astype(jnp.float32).reshape(A_KV_HEADS, rep))
    oa = (oa.astype(jnp.float32) * keep[..., None]).reshape(b, L, A_Q_HEADS * HEAD_DIM)
    qb = rms_norm(qb.reshape(b, L, B_HEADS, HEAD_DIM), q_norm_b)
    kb = rms_norm(kb.reshape(b, L, B_HEADS, HEAD_DIM), k_norm_b)
    vb = vb.reshape(b, L, B_HEADS, HEAD_DIM)
    outs, lses = [], []
    for window, dilation in B_BRANCHES:
        o, l = dilated_branch(qb, kb, vb, slopes[A_Q_HEADS:], window, dilation)
        outs.append(o)
        lses.append(l)
    wts = jax.nn.softmax(jnp.stack(lses), axis=0)
    ob = jnp.einsum("gblh,gblhd->blhd", wts, jnp.stack(outs).astype(jnp.float32))
    ob = ob.reshape(b, L, B_HEADS * HEAD_DIM)
    return jnp.concatenate([oa, ob], axis=-1).astype(h.dtype) @ w_out


def s5_mixer(u, lam_re, lam_im, log_dt, b_re, b_im, c_re, c_im, d_skip, glu_w, glu_b):
    f32 = jnp.float32
    bsz, L, _ = u.shape
    uf = u.astype(f32).reshape(bsz, L, S5_GROUPS, S5_GROUP)
    lr, li = lam_re.astype(f32), lam_im.astype(f32)
    dt = jnp.exp(log_dt.astype(f32))[:, None]
    mag, ang = jnp.exp(lr * dt), li * dt
    ab_re, ab_im = mag * jnp.cos(ang), mag * jnp.sin(ang)
    nr, ni = ab_re - 1.0, ab_im
    den = lr * lr + li * li
    f_re = (nr * lr + ni * li) / den
    f_im = (ni * lr - nr * li) / den
    bu_re = jnp.einsum("blgi,gpi->blgp", uf, b_re.astype(f32))
    bu_im = jnp.einsum("blgi,gpi->blgp", uf, b_im.astype(f32))
    e_re = f_re * bu_re - f_im * bu_im
    e_im = f_re * bu_im + f_im * bu_re
    a_re = jnp.broadcast_to(ab_re, e_re.shape)
    a_im = jnp.broadcast_to(ab_im, e_im.shape)

    def combine(e1, e2):
        a1r, a1i, b1r, b1i = e1
        a2r, a2i, b2r, b2i = e2
        return (a2r * a1r - a2i * a1i, a2r * a1i + a2i * a1r,
                a2r * b1r - a2i * b1i + b2r, a2r * b1i + a2i * b1r + b2i)

    _, _, x_re, x_im = lax.associative_scan(combine, (a_re, a_im, e_re, e_im), axis=1)
    y = (jnp.einsum("blgp,gip->blgi", x_re, c_re.astype(f32))
         - jnp.einsum("blgp,gip->blgi", x_im, c_im.astype(f32))
         + d_skip.astype(f32).reshape(S5_GROUPS, S5_GROUP) * uf)
    g = jax.nn.gelu(y.reshape(bsz, L, S5_WIDTH))
    return (g * jax.nn.sigmoid(g @ glu_w.astype(f32) + glu_b.astype(f32))).astype(u.dtype)


def chunk_gated_delta_rule(q, k, v, g, beta):
    b, L, H, dk = q.shape
    dv = v.shape[-1]
    n, C = L // DN_CHUNK, DN_CHUNK

    def chunks(t):
        return jnp.moveaxis(t.reshape((b, n, C) + t.shape[2:]), 3, 2)

    q, k, v, g, beta = chunks(q), chunks(k), chunks(v), chunks(g), chunks(beta)
    G = jnp.cumsum(g, axis=-1)
    causal = jnp.tril(jnp.ones((C, C), bool))
    strict = jnp.tril(jnp.ones((C, C), bool), -1)
    diff = G[..., :, None] - G[..., None, :]
    gamma = jnp.where(causal, jnp.exp(jnp.where(causal, diff, 0.0)), 0.0)
    kk = jnp.einsum("bnhid,bnhjd->bnhij", k, k)
    n_mat = jnp.where(strict, beta[..., :, None] * kk * gamma, 0.0)
    rhs = jnp.concatenate([v * beta[..., None], k * (beta * jnp.exp(G))[..., None]], axis=-1)
    sol = lax.linalg.triangular_solve(n_mat + jnp.eye(C, dtype=jnp.float32), rhs,
                                      left_side=True, lower=True, unit_diagonal=True)
    u, w = sol[..., :dv], sol[..., dv:]
    qk = jnp.einsum("bnhid,bnhjd->bnhij", q, k) * gamma
    q_dec = q * jnp.exp(G)[..., None]
    k_dec = k * jnp.exp(G[..., -1:] - G)[..., None]
    g_last = jnp.exp(G[..., -1])

    def step(S, xs):
        u_c, w_c, qk_c, qd_c, kd_c, gl_c = xs
        v_new = u_c - jnp.einsum("bhcd,bhde->bhce", w_c, S)
        o = jnp.einsum("bhcd,bhde->bhce", qd_c, S) + jnp.einsum("bhij,bhje->bhie", qk_c, v_new)
        S = S * gl_c[..., None, None] + jnp.einsum("bhcd,bhce->bhde", kd_c, v_new)
        return S, o

    xs = tuple(jnp.moveaxis(t, 1, 0) for t in (u, w, qk, q_dec, k_dec, g_last))
    _, o = lax.scan(step, jnp.zeros((b, H, dk, dv), jnp.float32), xs)
    return jnp.moveaxis(jnp.moveaxis(o, 0, 1), 2, 3).reshape(b, L, H, dv)


def gated_deltanet_mixer(q, k, v, z, a, beta_raw, conv_w, a_log, dt_bias, out_norm):
    f32 = jnp.float32
    bsz, L, _ = q.shape
    qkv = jax.nn.silu(causal_dwconv(jnp.concatenate([q, k, v], axis=-1), conv_w)).astype(f32)
    q, k, v = split_cols(qkv, [DN_HEADS * DN_DK, DN_HEADS * DN_DK, DN_HEADS * DN_DV])
    q = q.reshape(bsz, L, DN_HEADS, DN_DK)
    k = k.reshape(bsz, L, DN_HEADS, DN_DK)
    v = v.reshape(bsz, L, DN_HEADS, DN_DV)
    q = q * lax.rsqrt(jnp.sum(q * q, axis=-1, keepdims=True) + EPS) * (DN_DK ** -0.5)
    k = k * lax.rsqrt(jnp.sum(k * k, axis=-1, keepdims=True) + EPS)
    beta = jax.nn.sigmoid(beta_raw.astype(f32))
    g = -jnp.exp(a_log.astype(f32)) * jax.nn.softplus(a.astype(f32) + dt_bias.astype(f32))
    o = chunk_gated_delta_rule(q, k, v, g, beta)
    o = rms_norm(o, out_norm) * jax.nn.silu(z.astype(f32).reshape(bsz, L, DN_HEADS, DN_DV))
    return o.reshape(bsz, L, DN_HEADS * DN_DV)


def recurrent_mixer(h, w_in, lam_re, lam_im, log_dt, b_re, b_im, c_re, c_im, d_skip, glu_w,
                    glu_b, dn_conv, a_log, dt_bias, out_norm, w_out):
    u, q, k, v, z, a, beta_raw = split_cols(h @ w_in, REC_SPLITS)
    yc = s5_mixer(u, lam_re, lam_im, log_dt, b_re, b_im, c_re, c_im, d_skip, glu_w, glu_b)
    yd = gated_deltanet_mixer(q, k, v, z, a, beta_raw, dn_conv, a_log, dt_bias, out_norm)
    return jnp.concatenate([yc.astype(h.dtype), yd.astype(h.dtype)], axis=-1) @ w_out


def conv_ffn(h, w_up, conv_w, w_down):
    up = causal_dwconv(h @ w_up, conv_w)
    a, b = jnp.split(up, 2, axis=-1)
    return (jax.nn.silu(a) * b) @ w_down


def modulate(x, norm_w, shift, scale):
    return rms_norm(x, norm_w) * (1.0 + scale[:, None, :]) + shift[:, None, :]


def setup_inputs(seed: int = 0) -> dict:
    key = jax.random.key(seed)
    ks = iter(jax.random.split(key, 40))
    f32 = jnp.float32
    na, nr = N_ATTN_LAYERS, N_REC_LAYERS

    def nrm(shape, scale):
        return jax.random.normal(next(ks), shape, f32) * scale

    def gain(shape):
        return 1.0 + nrm(shape, 0.02)

    x = nrm((BATCH, SEQ, D_MODEL), 1.0)
    c = nrm((BATCH, D_MODEL), 1.0)
    ada_w = nrm((DEPTH, D_MODEL, 6 * D_MODEL), 0.5 * D_MODEL ** -0.5)
    ada_b = nrm((DEPTH, 6 * D_MODEL), 0.02)
    norm_mix = gain((DEPTH, D_MODEL))
    norm_ffn = gain((DEPTH, D_MODEL))
    attn_w_in = nrm((na, D_MODEL, ATTN_IN), D_MODEL ** -0.5)
    attn_q_norm_a = gain((na, HEAD_DIM))
    attn_k_norm_a = gain((na, HEAD_DIM))
    attn_q_norm_b = gain((na, HEAD_DIM))
    attn_k_norm_b = gain((na, HEAD_DIM))
    attn_sinks = nrm((na, A_Q_HEADS), 1.0)
    attn_w_out = nrm((na, ATTN_OUT, D_MODEL), ATTN_OUT ** -0.5)
    rec_w_in = nrm((nr, D_MODEL, REC_IN), D_MODEL ** -0.5)
    s5_lambda_re = -0.5 + nrm((nr, S5_GROUPS, S5_STATE), 0.01)
    s5_lambda_im = jnp.pi * jnp.arange(S5_STATE, dtype=f32) + nrm((nr, S5_GROUPS, S5_STATE), 0.01)
    s5_log_dt = jax.random.uniform(next(ks), (nr, S5_GROUPS), f32, math.log(1e-3), math.log(1e-1))
    s5_b_re = nrm((nr, S5_GROUPS, S5_STATE, S5_GROUP), (2 * S5_GROUP) ** -0.5)
    s5_b_im = nrm((nr, S5_GROUPS, S5_STATE, S5_GROUP), (2 * S5_GROUP) ** -0.5)
    s5_c_re = nrm((nr, S5_GROUPS, S5_GROUP, S5_STATE), S5_STATE ** -0.5)
    s5_c_im = nrm((nr, S5_GROUPS, S5_GROUP, S5_STATE), S5_STATE ** -0.5)
    s5_d = nrm((nr, S5_WIDTH), 1.0)
    s5_glu_w = nrm((nr, S5_WIDTH, S5_WIDTH), S5_WIDTH ** -0.5)
    s5_glu_b = nrm((nr, S5_WIDTH), 0.02)
    dn_conv = nrm((nr, DN_CONV, DN_HEADS * (2 * DN_DK + DN_DV)), DN_CONV ** -0.5)
    dn_a_log = jnp.log(jax.random.uniform(next(ks), (nr, DN_HEADS), f32, 1.0, 16.0))
    dt0 = jnp.exp(jax.random.uniform(next(ks), (nr, DN_HEADS), f32, math.log(1e-3), math.log(1e-1)))
    dn_dt_bias = dt0 + jnp.log(-jnp.expm1(-dt0))
    dn_out_norm = gain((nr, DN_DV))
    rec_w_out = nrm((nr, REC_OUT, D_MODEL), REC_OUT ** -0.5)
    ffn_w_up = nrm((DEPTH, D_MODEL, 2 * D_FF), D_MODEL ** -0.5)
    ffn_conv = nrm((DEPTH, FFN_CONV, 2 * D_FF), FFN_CONV ** -0.5)
    ffn_w_down = nrm((DEPTH, D_FF, D_MODEL), D_FF ** -0.5)
    return {"x": x, "c": c, "ada_w": ada_w, "ada_b": ada_b, "norm_mix": norm_mix,
            "norm_ffn": norm_ffn, "attn_w_in": attn_w_in, "attn_q_norm_a": attn_q_norm_a,
            "attn_k_norm_a": attn_k_norm_a, "attn_q_norm_b": attn_q_norm_b,
            "attn_k_norm_b": attn_k_norm_b, "attn_sinks": attn_sinks, "attn_w_out": attn_w_out,
            "rec_w_in": rec_w_in, "s5_lambda_re": s5_lambda_re, "s5_lambda_im": s5_lambda_im,
            "s5_log_dt": s5_log_dt, "s5_b_re": s5_b_re, "s5_b_im": s5_b_im, "s5_c_re": s5_c_re,
            "s5_c_im": s5_c_im, "s5_d": s5_d, "s5_glu_w": s5_glu_w, "s5_glu_b": s5_glu_b,
            "dn_conv": dn_conv, "dn_a_log": dn_a_log, "dn_dt_bias": dn_dt_bias,
            "dn_out_norm": dn_out_norm, "rec_w_out": rec_w_out, "ffn_w_up": ffn_w_up,
            "ffn_conv": ffn_conv, "ffn_w_down": ffn_w_down}


def reference(x, c, ada_w, ada_b, norm_mix, norm_ffn, attn_w_in, attn_q_norm_a, attn_k_norm_a,
              attn_q_norm_b, attn_k_norm_b, attn_sinks, attn_w_out, rec_w_in, s5_lambda_re,
              s5_lambda_im, s5_log_dt, s5_b_re, s5_b_im, s5_c_re, s5_c_im, s5_d, s5_glu_w,
              s5_glu_b, dn_conv, dn_a_log, dn_dt_bias, dn_out_norm, rec_w_out, ffn_w_up,
              ffn_conv, ffn_w_down):
    cond = jax.nn.silu(c)
    for layer in range(DEPTH):
        mod = cond @ ada_w[layer] + ada_b[layer]
        sh1, sc1, g1, sh2, sc2, g2 = jnp.split(mod, 6, axis=-1)
        h = modulate(x, norm_mix[layer], sh1, sc1)
        i = layer // 2
        if layer % 2 == 0:
            y = attention_mixer(h, attn_w_in[i], attn_q_norm_a[i], attn_k_norm_a[i],
                                attn_q_norm_b[i], attn_k_norm_b[i], attn_sinks[i], attn_w_out[i])
        else:
            y = recurrent_mixer(h, rec_w_in[i], s5_lambda_re[i], s5_lambda_im[i], s5_log_dt[i],
                                s5_b_re[i], s5_b_im[i], s5_c_re[i], s5_c_im[i], s5_d[i],
                                s5_glu_w[i], s5_glu_b[i], dn_conv[i], dn_a_log[i], dn_dt_bias[i],
                                dn_out_norm[i], rec_w_out[i])
        x = x + g1[:, None, :] * y
        h = modulate(x, norm_ffn[layer], sh2, sc2)
        x = x + g2[:, None, :] * conv_ffn(h, ffn_w_up[layer], ffn_conv[layer], ffn_w_down[layer])
    return x
```

```python
import functools
import math

import numpy as np
import jax
import jax.numpy as jnp
from jax import lax
from jax.experimental import pallas as pl
from jax.experimental.pallas import tpu as pltpu

F32 = jnp.float32
BF16 = jnp.bfloat16
HIGHEST = lax.Precision.HIGHEST
EPS = 1e-6
NEG = -1e30

HEAD_DIM = 64
BLOCK = 128
A_Q_HEADS = 8
A_KV_HEADS = 2
A_WINDOW = 128
B_HEADS = 8
B_BRANCHES = ((128, 1), (512, 4), (2048, 16))
N_ATTN_HEADS = A_Q_HEADS + B_HEADS
S5_GROUP = 16
S5_GROUPS = 16
S5_WIDTH = 256
S5_STATE = 64
DN_HEADS = 6
DN_DK = 128
DN_CONV = 4
DN_CHUNK = 64
FFN_CONV = 3

V7X_VMEM_LIMIT = 60 * 1024 * 1024
LANE = 128


def _cparams(sem, vmem=V7X_VMEM_LIMIT):
    return pltpu.CompilerParams(dimension_semantics=sem, vmem_limit_bytes=vmem)


def _resident(shape):
    n = len(shape)
    return pl.BlockSpec(shape, lambda *_: (0,) * n, pipeline_mode=pl.Buffered(1))


def _sigmoid(x):
    return 1.0 / (1.0 + jnp.exp(-x))


def _silu(x):
    return x * _sigmoid(x)


def _modulate(x, nw, scale, shift):
    ms = jnp.mean(x * x, axis=-1, keepdims=True)
    y = x * lax.rsqrt(ms + EPS) * nw
    return y * (1.0 + scale) + shift


def _adaln_kernel(c_ref, w_ref, b_ref, o_ref):
    c = c_ref[...]
    o_ref[0] = jnp.dot(_silu(c), w_ref[0], precision=HIGHEST, preferred_element_type=F32) + b_ref[0]


def adaln(c, ada_w, ada_b, tn=1536):
    depth, d, n = ada_w.shape
    bsz = c.shape[0]
    cp = jnp.zeros((8, d), F32).at[:bsz].set(c)
    out = pl.pallas_call(
        _adaln_kernel,
        grid=(depth, n // tn),
        in_specs=[pl.BlockSpec((8, d), lambda l, j: (0, 0)),
                  pl.BlockSpec((1, d, tn), lambda l, j: (l, 0, j)),
                  pl.BlockSpec((1, 1, tn), lambda l, j: (l, 0, j))],
        out_specs=pl.BlockSpec((1, 8, tn), lambda l, j: (l, 0, j)),
        out_shape=jax.ShapeDtypeStruct((depth, 8, n), F32),
        compiler_params=_cparams(("arbitrary", "arbitrary")),
        name="adaln",
    )(cp, ada_w, ada_b.reshape(depth, 1, n))
    return out[:, :bsz]


def _inproj_kernel(x_ref, nw_ref, sh_ref, sc_ref, w_ref, o_ref, *, ncol):
    h = _modulate(x_ref[...], nw_ref[...], sc_ref[0], sh_ref[0]).astype(BF16)
    nout = o_ref.shape[-1]
    for j in range(nout // ncol):
        o_ref[:, j * ncol:(j + 1) * ncol] = jnp.dot(h, w_ref[:, j * ncol:(j + 1) * ncol],
                                                    preferred_element_type=F32)


def inproj(x2, seq, nw, shift, scale, w, tm=512, ncol=1152):
    n, d = x2.shape
    nout = w.shape[1]
    per = seq // tm
    return pl.pallas_call(
        functools.partial(_inproj_kernel, ncol=ncol),
        grid=(n // tm,),
        in_specs=[pl.BlockSpec((tm, d), lambda i: (i, 0)),
                  _resident((1, d)),
                  pl.BlockSpec((1, 1, d), lambda i: (i // per, 0, 0)),
                  pl.BlockSpec((1, 1, d), lambda i: (i // per, 0, 0)),
                  _resident((d, nout))],
        out_specs=pl.BlockSpec((tm, nout), lambda i: (i, 0)),
        out_shape=jax.ShapeDtypeStruct((n, nout), F32),
        compiler_params=_cparams(("parallel",)),
        name="inproj",
    )(x2, nw.reshape(1, d), shift, scale, w)


def _band_kernel(slopes_ref, sinks_ref, q_ref, k_ref, v_ref, kh_ref, vh_ref, qw_ref, kw_ref, *rest,
                 nq, shared_kv, head0, step, max_dist, with_lse, rchunk):
    if with_lse:
        o_ref, lse_ref, qs, ks, vs, bias_s = rest
    else:
        o_ref, qs, ks, vs, bias_s = rest
        lse_ref = None
    u = pl.program_id(1)
    t = pl.program_id(3)
    tq = q_ref.shape[1]
    nkv = 1 if shared_kv else nq

    qi = lax.broadcasted_iota(jnp.int32, (BLOCK, 2 * BLOCK), 0)
    si = lax.broadcasted_iota(jnp.int32, (BLOCK, 2 * BLOCK), 1)
    dist = BLOCK + qi - si
    valid = (dist >= 0) & (dist <= max_dist)
    valid0 = valid & (si >= BLOCK)
    distf = (step * dist).astype(F32)
    for j in range(nq):
        b = -slopes_ref[head0 + u * nq + j] * distf
        bias_s[0, j] = jnp.where(valid0, b, NEG)
        bias_s[1, j] = jnp.where(valid, b, NEG)

    def norm_rows(x, w):
        ms = jnp.mean(x * x, axis=-1, keepdims=True)
        return x * lax.rsqrt(ms + EPS) * w

    def pick_kv(x):
        return jnp.where(u == 0, x[:, :HEAD_DIM], x[:, HEAD_DIM:])

    qw = qw_ref[...]
    kw = kw_ref[...]

    def prep(c, _):
        r0 = pl.multiple_of(c * rchunk, rchunk)
        xq = q_ref[0, pl.ds(r0, rchunk), :]
        for j in range(nq):
            qn = norm_rows(xq[:, j * HEAD_DIM:(j + 1) * HEAD_DIM], qw) * (HEAD_DIM ** -0.5)
            qs[j, pl.ds(r0, rchunk), :] = qn.astype(BF16)
        xk = k_ref[0, pl.ds(r0, rchunk), :]
        xv = v_ref[0, pl.ds(r0, rchunk), :]
        r1 = pl.multiple_of(BLOCK + c * rchunk, rchunk)
        if shared_kv:
            ks[0, pl.ds(r1, rchunk), :] = norm_rows(pick_kv(xk), kw).astype(BF16)
            vs[0, pl.ds(r1, rchunk), :] = pick_kv(xv).astype(BF16)
        else:
            for j in range(nq):
                ks[j, pl.ds(r1, rchunk), :] = norm_rows(xk[:, j * HEAD_DIM:(j + 1) * HEAD_DIM], kw).astype(BF16)
                vs[j, pl.ds(r1, rchunk), :] = xv[:, j * HEAD_DIM:(j + 1) * HEAD_DIM].astype(BF16)
        return 0

    lax.fori_loop(0, tq // rchunk, prep, 0)
    xk = kh_ref[0]
    xv = vh_ref[0]
    if shared_kv:
        ks[0, 0:BLOCK, :] = norm_rows(pick_kv(xk), kw).astype(BF16)
        vs[0, 0:BLOCK, :] = pick_kv(xv).astype(BF16)
    else:
        for j in range(nq):
            ks[j, 0:BLOCK, :] = norm_rows(xk[:, j * HEAD_DIM:(j + 1) * HEAD_DIM], kw).astype(BF16)
            vs[j, 0:BLOCK, :] = xv[:, j * HEAD_DIM:(j + 1) * HEAD_DIM].astype(BF16)

    def blk(n, _):
        r0 = pl.multiple_of(n * BLOCK, BLOCK)
        bi = jnp.where(jnp.logical_and(t == 0, n == 0), 0, 1)
        outs, lses = [], []
        for j in range(nq):
            kvj = 0 if shared_kv else j
            q = qs[j, pl.ds(r0, BLOCK), :]
            kk = ks[kvj, pl.ds(r0, 2 * BLOCK), :]
            vv = vs[kvj, pl.ds(r0, 2 * BLOCK), :]
            s = lax.dot_general(q, kk, (((1,), (1,)), ((), ())), preferred_element_type=F32)
            s = s + bias_s[bi, j]
            m = jnp.max(s, axis=-1, keepdims=True)
            p = jnp.exp(s - m)
            l = jnp.sum(p, axis=-1, keepdims=True)
            acc = jnp.dot(p.astype(BF16), vv, preferred_element_type=F32)
            if with_lse:
                outs.append(acc / l)
                lses.append(jnp.broadcast_to(m + jnp.log(l), (BLOCK, HEAD_DIM)))
            else:
                outs.append(acc / (l + jnp.exp(sinks_ref[u * nq + j] - m)))
        o_ref[0, pl.ds(r0, BLOCK), :] = jnp.concatenate(outs, axis=-1)
        if with_lse:
            lse_ref[0, pl.ds(r0, BLOCK), :] = jnp.concatenate(lses, axis=-1)
        return 0

    lax.fori_loop(0, tq // BLOCK, blk, 0)


def band_attention(qkv, slopes, sinks, qw, kw, *, mixer, dilation=1, window=A_WINDOW):
    bsz, seq, ncols = qkv.shape
    d = dilation
    ls = seq // d
    view = qkv.reshape(bsz, ls, d * ncols)
    cb = ncols // LANE
    tq = min(ls, 2048)
    nt = ls // tq
    hb = tq // BLOCK
    if mixer == "a":
        nq, units, shared = A_Q_HEADS // A_KV_HEADS, A_KV_HEADS, True
        qwid = nq * HEAD_DIM
        qmap = lambda b, u, r, t: (b, t, u)
        kcol, vcol = 4, 5
        kmap = lambda b, u, r, t: (b, t, kcol)
        vmap = lambda b, u, r, t: (b, t, vcol)
        khmap = lambda b, u, r, t: (b, jnp.maximum(t * hb - 1, 0), kcol)
        vhmap = lambda b, u, r, t: (b, jnp.maximum(t * hb - 1, 0), vcol)
        omap = lambda b, u, r, t: (b, t, u)
        head0, step, max_dist, with_lse = 0, 1, window - 1, False
        ocols = A_Q_HEADS * HEAD_DIM
    else:
        nq, units, shared = 2, B_HEADS // 2, False
        qwid = nq * HEAD_DIM
        qmap = lambda b, u, r, t: (b, t, r * cb + 6 + u)
        kmap = lambda b, u, r, t: (b, t, r * cb + 10 + u)
        vmap = lambda b, u, r, t: (b, t, r * cb + 14 + u)
        khmap = lambda b, u, r, t: (b, jnp.maximum(t * hb - 1, 0), r * cb + 10 + u)
        vhmap = lambda b, u, r, t: (b, jnp.maximum(t * hb - 1, 0), r * cb + 14 + u)
        omap = lambda b, u, r, t: (b, t, r * units + u)
        head0, step, max_dist, with_lse = A_Q_HEADS, d, window // d, True
        ocols = B_HEADS * HEAD_DIM
    nkv = 1 if shared else nq
    smem = pl.BlockSpec(memory_space=pltpu.SMEM)
    oshape = jax.ShapeDtypeStruct((bsz, ls, d * ocols), F32)
    ospec = pl.BlockSpec((1, tq, qwid), omap)
    outs = pl.pallas_call(
        functools.partial(_band_kernel, nq=nq, shared_kv=shared, head0=head0, step=step, max_dist=max_dist,
                          with_lse=with_lse, rchunk=256),
        grid=(bsz, units, d, nt),
        in_specs=[smem, smem,
                  pl.BlockSpec((1, tq, qwid), qmap),
                  pl.BlockSpec((1, tq, LANE), kmap),
                  pl.BlockSpec((1, tq, LANE), vmap),
                  pl.BlockSpec((1, BLOCK, LANE), khmap),
                  pl.BlockSpec((1, BLOCK, LANE), vhmap),
                  _resident((1, HEAD_DIM)), _resident((1, HEAD_DIM))],
        out_specs=(ospec, ospec) if with_lse else ospec,
        out_shape=(oshape, oshape) if with_lse else oshape,
        scratch_shapes=[pltpu.VMEM((nq, tq, HEAD_DIM), BF16),
                        pltpu.VMEM((nkv, BLOCK + tq, HEAD_DIM), BF16),
                        pltpu.VMEM((nkv, BLOCK + tq, HEAD_DIM), BF16),
                        pltpu.VMEM((2, nq, BLOCK, 2 * BLOCK), F32)],
        compiler_params=_cparams(("parallel", "parallel", "parallel", "arbitrary")),
        name=f"band_{mixer}{d}",
    )(slopes, sinks, view, view, view, view, view, qw.reshape(1, HEAD_DIM), kw.reshape(1, HEAD_DIM))
    if with_lse:
        return outs[0].reshape(bsz, seq, ocols), outs[1].reshape(bsz, seq, ocols)
    return outs.reshape(bsz, seq, ocols)


def _attn_out_kernel(x_ref, g_ref, oa_ref, o1_ref, o2_ref, o3_ref, l1_ref, l2_ref, l3_ref, w_ref, out_ref):
    l1, l2, l3 = l1_ref[...], l2_ref[...], l3_ref[...]
    m = jnp.maximum(jnp.maximum(l1, l2), l3)
    e1, e2, e3 = jnp.exp(l1 - m), jnp.exp(l2 - m), jnp.exp(l3 - m)
    ob = (e1 * o1_ref[...] + e2 * o2_ref[...] + e3 * o3_ref[...]) / (e1 + e2 + e3)
    half = oa_ref.shape[-1]
    y = jnp.dot(oa_ref[...].astype(BF16), w_ref[:half, :], preferred_element_type=F32)
    y = y + jnp.dot(ob.astype(BF16), w_ref[half:, :], preferred_element_type=F32)
    out_ref[...] = x_ref[...] + g_ref[0] * y


def attn_out(x2, seq, gate, oa, obs, lses, w, tm=512):
    n, d = x2.shape
    per = seq // tm
    half = oa.shape[-1]
    row = lambda i: (i, 0)
    hs = pl.BlockSpec((tm, half), row)
    return pl.pallas_call(
        _attn_out_kernel,
        grid=(n // tm,),
        in_specs=[pl.BlockSpec((tm, d), row), pl.BlockSpec((1, 1, d), lambda i: (i // per, 0, 0)),
                  hs, hs, hs, hs, hs, hs, hs, _resident(w.shape)],
        out_specs=pl.BlockSpec((tm, d), row),
        out_shape=jax.ShapeDtypeStruct((n, d), F32),
        compiler_params=_cparams(("parallel",)),
        name="attn_out",
    )(x2, gate, oa, *obs, *lses, w)


def _gelu_tanh(x):
    return 0.5 * x * (1.0 + jnp.tanh(math.sqrt(2.0 / math.pi) * (x + 0.044715 * (x * x * x))))


def _s5_kernel(u_ref, e_ref, pr_ref, pi_ref, cr_ref, ci_ref, d_ref, gw_ref, gb_ref, o_ref, xr_s, xi_s, car_s):
    t = pl.program_id(1)
    tc = u_ref.shape[1]
    ns = pr_ref.shape[1]

    @pl.when(t == 0)
    def _():
        car_s[...] = jnp.zeros_like(car_s)

    u = u_ref[0]
    e = jnp.dot(u.astype(BF16), e_ref[...], preferred_element_type=F32)
    xr, xi = e[:, :ns], e[:, ns:]
    row = lax.broadcasted_iota(jnp.int32, (tc, ns), 0) % 8
    for s in (1, 2, 4):
        ar, ai = pr_ref[s - 1:s, :], pi_ref[s - 1:s, :]
        keep = row >= s
        sr = jnp.where(keep, pltpu.roll(xr, s, 0), 0.0)
        si = jnp.where(keep, pltpu.roll(xi, s, 0), 0.0)
        xr, xi = xr + (ar * sr - ai * si), xi + (ar * si + ai * sr)
    xr_s[...] = xr
    xi_s[...] = xi
    pr, pi = pr_ref[...], pi_ref[...]

    def grp(k, carry):
        cr, ci = carry
        r0 = pl.multiple_of(k * 8, 8)
        nr = xr_s[pl.ds(r0, 8), :] + (pr * cr - pi * ci)
        ni = xi_s[pl.ds(r0, 8), :] + (pr * ci + pi * cr)
        xr_s[pl.ds(r0, 8), :] = nr
        xi_s[pl.ds(r0, 8), :] = ni
        return nr[7:8, :], ni[7:8, :]

    cr, ci = lax.fori_loop(0, tc // 8, grp, (car_s[0:1, :], car_s[1:2, :]))
    car_s[0:1, :] = cr
    car_s[1:2, :] = ci
    y = (jnp.dot(xr_s[...].astype(BF16), cr_ref[...], preferred_element_type=F32)
         - jnp.dot(xi_s[...].astype(BF16), ci_ref[...], preferred_element_type=F32)
         + d_ref[...] * u)
    g = _gelu_tanh(y)
    z = jnp.dot(g.astype(BF16), gw_ref[...], preferred_element_type=F32) + gb_ref[...]
    o_ref[0] = g * _sigmoid(z)


def s5_mixer(rec, lam_re, lam_im, log_dt, b_re, b_im, c_re, c_im, d_skip, glu_w, glu_b, tc=512):
    bsz, seq, _ = rec.shape
    ns = S5_GROUPS * S5_STATE
    dt = jnp.exp(log_dt)[:, None]
    lr, li = lam_re, lam_im
    mag, ang = jnp.exp(lr * dt), li * dt
    ab_re, ab_im = mag * jnp.cos(ang), mag * jnp.sin(ang)
    nr, ni = ab_re - 1.0, ab_im
    den = lr * lr + li * li
    f_re = (nr * lr + ni * li) / den
    f_im = (ni * lr - nr * li) / den
    kk = jnp.arange(1, 9, dtype=F32)[:, None, None]
    pmag = jnp.exp(kk * (lr * dt)[None])
    p_re = (pmag * jnp.cos(kk * ang[None])).reshape(8, ns)
    p_im = (pmag * jnp.sin(kk * ang[None])).reshape(8, ns)
    e_re = f_re[..., None] * b_re - f_im[..., None] * b_im
    e_im = f_re[..., None] * b_im + f_im[..., None] * b_re
    eye = jnp.eye(S5_GROUPS, dtype=F32)

    def bd_in(m):
        return jnp.einsum("gpi,gh->gihp", m, eye).reshape(S5_WIDTH, ns)

    def bd_out(m):
        return jnp.einsum("gip,gh->gphi", m, eye).reshape(ns, S5_WIDTH)

    e_mat = jnp.concatenate([bd_in(e_re), bd_in(e_im)], axis=1).astype(BF16)
    return pl.pallas_call(
        _s5_kernel,
        grid=(bsz, seq // tc),
        in_specs=[pl.BlockSpec((1, tc, S5_WIDTH), lambda b, t: (b, t, 0)),
                  _resident((S5_WIDTH, 2 * ns)), _resident((8, ns)), _resident((8, ns)),
                  _resident((ns, S5_WIDTH)), _resident((ns, S5_WIDTH)),
                  _resident((1, S5_WIDTH)), _resident((S5_WIDTH, S5_WIDTH)), _resident((1, S5_WIDTH))],
        out_specs=pl.BlockSpec((1, tc, S5_WIDTH), lambda b, t: (b, t, 0)),
        out_shape=jax.ShapeDtypeStruct((bsz, seq, S5_WIDTH), F32),
        scratch_shapes=[pltpu.VMEM((tc, ns), F32), pltpu.VMEM((tc, ns), F32), pltpu.VMEM((8, ns), F32)],
        compiler_params=_cparams(("parallel", "arbitrary")),
        name="s5",
    )(rec, e_mat, p_re, p_im, bd_out(c_re).astype(BF16), bd_out(c_im).astype(BF16),
      d_skip.reshape(1, S5_WIDTH), glu_w.astype(BF16), glu_b.reshape(1, S5_WIDTH))


def _dot_hi(a, b):
    return jnp.dot(a, b, precision=HIGHEST, preferred_element_type=F32)


def _dot_nt_hi(a, b):
    return lax.dot_general(a, b, (((1,), (1,)), ((), ())), precision=HIGHEST, preferred_element_type=F32)


def _unit_lower_inverse(nm):
    c = nm.shape[0]
    eye = (lax.broadcasted_iota(jnp.int32, (c, c), 0) == lax.broadcasted_iota(jnp.int32, (c, c), 1)).astype(F32)
    inv = eye - nm
    pw = _dot_hi(nm, nm)
    span = 2
    while span < c:
        inv = inv + _dot_hi(inv, pw)
        span *= 2
        if span < c:
            pw = _dot_hi(pw, pw)
    return inv


def _dn_kernel(alog_ref, dtb_ref, q_ref, k_ref, v_ref, qh_ref, kh_ref, vh_ref, z_ref, ab_ref,
               wq_ref, wk_ref, wv_ref, nw_ref, o_ref, s_s):
    h = pl.program_id(1)
    t = pl.program_id(2)
    tc = q_ref.shape[1]
    c = DN_CHUNK

    @pl.when(t == 0)
    def _():
        s_s[...] = jnp.zeros_like(s_s)

    live = (t > 0).astype(F32)

    def conv_silu(x_ref, halo_ref, w_ref):
        xe = jnp.concatenate([halo_ref[0] * live, x_ref[0]], axis=0)
        w = w_ref[...]
        y = w[DN_CONV - 1:DN_CONV, :] * xe[8:, :]
        for s in range(1, DN_CONV):
            y = y + w[DN_CONV - 1 - s:DN_CONV - s, :] * pltpu.roll(xe, s, 0)[8:, :]
        return _silu(y)

    q = conv_silu(q_ref, qh_ref, wq_ref)
    k = conv_silu(k_ref, kh_ref, wk_ref)
    v = conv_silu(v_ref, vh_ref, wv_ref)
    q = q * lax.rsqrt(jnp.sum(q * q, axis=-1, keepdims=True) + EPS) * (DN_DK ** -0.5)
    k = k * lax.rsqrt(jnp.sum(k * k, axis=-1, keepdims=True) + EPS)

    ab = ab_ref[0]
    lane = lax.broadcasted_iota(jnp.int32, ab.shape, 1)
    a = jnp.sum(jnp.where(lane == h, ab, 0.0), axis=-1, keepdims=True)
    braw = jnp.sum(jnp.where(lane == DN_HEADS + h, ab, 0.0), axis=-1, keepdims=True)
    beta = _sigmoid(braw)
    xa = a + dtb_ref[h]
    softplus = jnp.maximum(xa, 0.0) + jnp.log(1.0 + jnp.exp(-jnp.abs(xa)))
    g = -jnp.exp(alog_ref[h]) * softplus

    ri = lax.broadcasted_iota(jnp.int32, (c, c), 0)
    ci = lax.broadcasted_iota(jnp.int32, (c, c), 1)
    causal = ri >= ci
    strict = ri > ci
    tril = causal.astype(F32)
    triu = (ri <= ci).astype(F32)
    ones = jnp.ones((c, c), F32)

    s_state = s_s[...]
    nw = nw_ref[...]
    for i in range(tc // c):
        sl = slice(i * c, (i + 1) * c)
        qc, kc, vc, bc = q[sl], k[sl], v[sl], beta[sl]
        gb = jnp.broadcast_to(g[sl], (c, c))
        gcol = _dot_hi(tril, gb)
        grow = _dot_hi(ones, gb * triu)
        gi = gcol[:, 0:1]
        glast = gcol[c - 1:c, 0:1]
        gamma = jnp.where(causal, jnp.exp(jnp.where(causal, gcol - grow, 0.0)), 0.0)
        kkm = _dot_nt_hi(kc, kc)
        nm = jnp.where(strict, bc * kkm * gamma, 0.0)
        tinv = _unit_lower_inverse(nm)
        eg = jnp.exp(gi)
        rhs = jnp.concatenate([vc * bc, kc * (bc * eg)], axis=-1)
        sol = _dot_hi(tinv, rhs)
        uc, wc = sol[:, :DN_DK], sol[:, DN_DK:]
        qk = _dot_nt_hi(qc, kc) * gamma
        qd = qc * eg
        kd = kc * jnp.exp(glast - gi)
        v_new = uc - _dot_hi(wc, s_state)
        o = _dot_hi(qd, s_state) + _dot_hi(qk, v_new)
        s_state = s_state * jnp.exp(glast) + lax.dot_general(
            kd, v_new, (((0,), (0,)), ((), ())), precision=HIGHEST, preferred_element_type=F32)
        zc = z_ref[0, sl, :]
        on = o * lax.rsqrt(jnp.mean(o * o, axis=-1, keepdims=True) + EPS) * nw
        o_ref[0, sl, :] = on * _silu(zc)
    s_s[...] = s_state


def deltanet_mixer(rec, conv_w, a_log, dt_bias, out_norm, tc=256):
    bsz, seq, _ = rec.shape
    hb = tc // 8
    nh = DN_HEADS
    smem = pl.BlockSpec(memory_space=pltpu.SMEM)

    def col(c0):
        return pl.BlockSpec((1, tc, LANE), lambda b, h, t: (b, t, c0 + h))

    def halo(c0):
        return pl.BlockSpec((1, 8, LANE), lambda b, h, t: (b, jnp.maximum(t * hb - 1, 0), c0 + h))

    def wcol(c0):
        return pl.BlockSpec((DN_CONV, LANE), lambda b, h, t: (0, c0 + h))

    return pl.pallas_call(
        _dn_kernel,
        grid=(bsz, nh, seq // tc),
        in_specs=[smem, smem, col(2), col(8), col(14), halo(2), halo(8), halo(14), col(20),
                  pl.BlockSpec((1, tc, LANE), lambda b, h, t: (b, t, 26)),
                  wcol(0), wcol(nh), wcol(2 * nh), _resident((1, DN_DK))],
        out_specs=pl.BlockSpec((1, tc, LANE), lambda b, h, t: (b, t, h)),
        out_shape=jax.ShapeDtypeStruct((bsz, seq, nh * DN_DK), F32),
        scratch_shapes=[pltpu.VMEM((DN_DK, DN_DK), F32)],
        compiler_params=_cparams(("parallel", "parallel", "arbitrary")),
        name="deltanet",
    )(a_log, dt_bias, rec, rec, rec, rec, rec, rec, rec, rec, conv_w, conv_w, conv_w, out_norm.reshape(1, DN_DK))


def _rec_out_kernel(x_ref, g_ref, yc_ref, yd_ref, w_ref, out_ref):
    nc = yc_ref.shape[-1]
    y = jnp.dot(yc_ref[...].astype(BF16), w_ref[:nc, :], preferred_element_type=F32)
    y = y + jnp.dot(yd_ref[...].astype(BF16), w_ref[nc:, :], preferred_element_type=F32)
    out_ref[...] = x_ref[...] + g_ref[0] * y


def rec_out(x2, seq, gate, yc, yd, w, tm=512):
    n, d = x2.shape
    per = seq // tm
    row = lambda i: (i, 0)
    return pl.pallas_call(
        _rec_out_kernel,
        grid=(n // tm,),
        in_specs=[pl.BlockSpec((tm, d), row), pl.BlockSpec((1, 1, d), lambda i: (i // per, 0, 0)),
                  pl.BlockSpec((tm, yc.shape[-1]), row), pl.BlockSpec((tm, yd.shape[-1]), row), _resident(w.shape)],
        out_specs=pl.BlockSpec((tm, d), row),
        out_shape=jax.ShapeDtypeStruct((n, d), F32),
        compiler_params=_cparams(("parallel",)),
        name="rec_out",
    )(x2, gate, yc, yd, w)


FFN_HALO = 16


def _ffn_kernel(x_ref, xh_ref, nw_ref, sh_ref, sc_ref, g_ref, wu_ref, cw_ref, wd_ref, out_ref, hb_s, *, per):
    i = pl.program_id(0)
    tm = x_ref.shape[0]
    nf = wd_ref.shape[0]
    nw, sh, sc = nw_ref[...], sh_ref[0], sc_ref[0]
    x = x_ref[...]
    live = (i % per > 0).astype(F32)
    hb_s[0:FFN_HALO, :] = (_modulate(xh_ref[...], nw, sc, sh) * live).astype(BF16)
    hb_s[FFN_HALO:, :] = _modulate(x, nw, sc, sh).astype(BF16)
    hb = hb_s[...]

    def conv(up, w):
        y = w[FFN_CONV - 1:FFN_CONV, :] * up[FFN_HALO:, :]
        for s in range(1, FFN_CONV):
            y = y + w[FFN_CONV - 1 - s:FFN_CONV - s, :] * pltpu.roll(up, s, 0)[FFN_HALO:, :]
        return y

    acc = jnp.zeros((tm, out_ref.shape[-1]), F32)
    for f in range(nf):
        ua = jnp.dot(hb, wu_ref[0, f], preferred_element_type=F32)
        ub = jnp.dot(hb, wu_ref[1, f], preferred_element_type=F32)
        act = _silu(conv(ua, cw_ref[0, f])) * conv(ub, cw_ref[1, f])
        acc = acc + jnp.dot(act.astype(BF16), wd_ref[f], preferred_element_type=F32)
    out_ref[...] = x + g_ref[0] * acc


def conv_ffn(x2, seq, nw, shift, scale, gate, w_up, conv_w, w_down, tm=512, cw=256):
    n, d = x2.shape
    dff = w_down.shape[0]
    nf = dff // cw
    per = seq // tm
    wu = w_up.astype(BF16).reshape(d, 2, nf, cw).transpose(1, 2, 0, 3)
    cwt = conv_w.reshape(FFN_CONV, 2, nf, cw).transpose(1, 2, 0, 3)
    wd = w_down.astype(BF16).reshape(nf, cw, d)
    row = lambda i: (i, 0)
    bat = lambda i: (i // per, 0, 0)
    hpt = tm // FFN_HALO
    return pl.pallas_call(
        functools.partial(_ffn_kernel, per=per),
        grid=(n // tm,),
        in_specs=[pl.BlockSpec((tm, d), row),
                  pl.BlockSpec((FFN_HALO, d), lambda i: (jnp.maximum(i * hpt - 1, 0), 0)),
                  _resident((1, d)), pl.BlockSpec((1, 1, d), bat), pl.BlockSpec((1, 1, d), bat),
                  pl.BlockSpec((1, 1, d), bat),
                  _resident(wu.shape), _resident(cwt.shape), _resident(wd.shape)],
        out_specs=pl.BlockSpec((tm, d), row),
        out_shape=jax.ShapeDtypeStruct((n, d), F32),
        scratch_shapes=[pltpu.VMEM((FFN_HALO + tm, d), BF16)],
        compiler_params=_cparams(("parallel",)),
        name="ffn",
    )(x2, x2, nw.reshape(1, d), shift, scale, gate, wu, cwt, wd)


def _alibi_slopes(n):
    return jnp.asarray(2.0 ** (-8.0 * np.arange(1, n + 1) / n), dtype=F32)


def _pad_cols(w, mult):
    pad = (-w.shape[1]) % mult
    return jnp.pad(w, ((0, 0), (0, pad)))


def kernel(x, c, ada_w, ada_b, norm_mix, norm_ffn, attn_w_in, attn_q_norm_a, attn_k_norm_a, attn_q_norm_b,
           attn_k_norm_b, attn_sinks, attn_w_out, rec_w_in, s5_lambda_re, s5_lambda_im, s5_log_dt, s5_b_re,
           s5_b_im, s5_c_re, s5_c_im, s5_d, s5_glu_w, s5_glu_b, dn_conv, dn_a_log, dn_dt_bias, dn_out_norm,
           rec_w_out, ffn_w_up, ffn_conv, ffn_w_down):
    bsz, seq, d = x.shape
    depth = ada_w.shape[0]
    mod = adaln(c, ada_w, ada_b)
    x2 = x.reshape(bsz * seq, d)
    slopes = _alibi_slopes(N_ATTN_HEADS)
    for layer in range(depth):
        sh1, sc1, g1, sh2, sc2, g2 = [mod[layer, :, j * d:(j + 1) * d].reshape(bsz, 1, d) for j in range(6)]
        i = layer // 2
        if layer % 2 == 0:
            qkv = inproj(x2, seq, norm_mix[layer], sh1, sc1, attn_w_in[i].astype(BF16))
            qkv = qkv.reshape(bsz, seq, -1)
            oa = band_attention(qkv, slopes, attn_sinks[i], attn_q_norm_a[i], attn_k_norm_a[i], mixer="a")
            obs, lses = [], []
            for window, dilation in B_BRANCHES:
                o, l = band_attention(qkv, slopes, attn_sinks[i], attn_q_norm_b[i], attn_k_norm_b[i],
                                      mixer="b", dilation=dilation, window=window)
                obs.append(o.reshape(bsz * seq, -1))
                lses.append(l.reshape(bsz * seq, -1))
            x2 = attn_out(x2, seq, g1, oa.reshape(bsz * seq, -1), obs, lses, attn_w_out[i].astype(BF16))
        else:
            rec = inproj(x2, seq, norm_mix[layer], sh1, sc1, _pad_cols(rec_w_in[i], LANE).astype(BF16))
            rec = rec.reshape(bsz, seq, -1)
            yc = s5_mixer(rec, s5_lambda_re[i], s5_lambda_im[i], s5_log_dt[i], s5_b_re[i], s5_b_im[i],
                          s5_c_re[i], s5_c_im[i], s5_d[i], s5_glu_w[i], s5_glu_b[i])
            yd = deltanet_mixer(rec, dn_conv[i], dn_a_log[i], dn_dt_bias[i], dn_out_norm[i])
            x2 = rec_out(x2, seq, g1, yc.reshape(bsz * seq, -1), yd.reshape(bsz * seq, -1),
                         rec_w_out[i].astype(BF16))
        x2 = conv_ffn(x2, seq, norm_ffn[layer], sh2, sc2, g2, ffn_w_up[layer], ffn_conv[layer], ffn_w_down[layer])
    return x2.reshape(bsz, seq, d)
```

```python
import functools
import math

import numpy as np
import jax
import jax.numpy as jnp
from jax import lax
from jax.experimental import pallas as pl
from jax.experimental.pallas import tpu as pltpu

F32 = jnp.float32
BF16 = jnp.bfloat16
HIGHEST = lax.Precision.HIGHEST
EPS = 1e-6
NEG = -1e30

HEAD_DIM = 64
BLOCK = 128
A_Q_HEADS = 8
A_KV_HEADS = 2
A_WINDOW = 128
B_HEADS = 8
B_BRANCHES = ((128, 1), (512, 4), (2048, 16))
N_ATTN_HEADS = A_Q_HEADS + B_HEADS
S5_GROUP = 16
S5_GROUPS = 16
S5_WIDTH = 256
S5_STATE = 64
DN_HEADS = 6
DN_DK = 128
DN_CONV = 4
DN_CHUNK = 64
DN_HPS = 2
FFN_CONV = 3

V7X_VMEM_LIMIT = 60 * 1024 * 1024
LANE = 128


def _cparams(sem, vmem=V7X_VMEM_LIMIT):
    return pltpu.CompilerParams(dimension_semantics=sem, vmem_limit_bytes=vmem)


def _resident(shape):
    n = len(shape)
    return pl.BlockSpec(shape, lambda *_: (0,) * n, pipeline_mode=pl.Buffered(1))


def _sigmoid(x):
    return 1.0 / (1.0 + jnp.exp(-x))


def _silu(x):
    return x * _sigmoid(x)


def _modulate(x, nw, scale, shift):
    ms = jnp.mean(x * x, axis=-1, keepdims=True)
    y = x * lax.rsqrt(ms + EPS) * nw
    return y * (1.0 + scale) + shift


def _adaln_kernel(c_ref, w_ref, b_ref, o_ref):
    c = c_ref[...]
    o_ref[0] = jnp.dot(_silu(c), w_ref[0], precision=HIGHEST, preferred_element_type=F32) + b_ref[0]


def adaln(c, ada_w, ada_b, tn=1536):
    depth, d, n = ada_w.shape
    bsz = c.shape[0]
    cp = jnp.zeros((8, d), F32).at[:bsz].set(c)
    out = pl.pallas_call(
        _adaln_kernel,
        grid=(depth, n // tn),
        in_specs=[pl.BlockSpec((8, d), lambda l, j: (0, 0)),
                  pl.BlockSpec((1, d, tn), lambda l, j: (l, 0, j)),
                  pl.BlockSpec((1, 1, tn), lambda l, j: (l, 0, j))],
        out_specs=pl.BlockSpec((1, 8, tn), lambda l, j: (l, 0, j)),
        out_shape=jax.ShapeDtypeStruct((depth, 8, n), F32),
        compiler_params=_cparams(("arbitrary", "arbitrary")),
        name="adaln",
    )(cp, ada_w, ada_b.reshape(depth, 1, n))
    return out[:, :bsz]


def _inproj_kernel(x_ref, nw_ref, sh_ref, sc_ref, w_ref, o_ref, *, ncol):
    h = _modulate(x_ref[...], nw_ref[...], sc_ref[0], sh_ref[0]).astype(BF16)
    nout = o_ref.shape[-1]
    for j in range(nout // ncol):
        o_ref[:, j * ncol:(j + 1) * ncol] = jnp.dot(h, w_ref[:, j * ncol:(j + 1) * ncol],
                                                    preferred_element_type=F32)


def inproj(x2, seq, nw, shift, scale, w, tm=512, ncol=1152):
    n, d = x2.shape
    nout = w.shape[1]
    per = seq // tm
    return pl.pallas_call(
        functools.partial(_inproj_kernel, ncol=ncol),
        grid=(n // tm,),
        in_specs=[pl.BlockSpec((tm, d), lambda i: (i, 0)),
                  _resident((1, d)),
                  pl.BlockSpec((1, 1, d), lambda i: (i // per, 0, 0)),
                  pl.BlockSpec((1, 1, d), lambda i: (i // per, 0, 0)),
                  _resident((d, nout))],
        out_specs=pl.BlockSpec((tm, nout), lambda i: (i, 0)),
        out_shape=jax.ShapeDtypeStruct((n, nout), F32),
        compiler_params=_cparams(("parallel",)),
        name="inproj",
    )(x2, nw.reshape(1, d), shift, scale, w)


def _band_kernel(slopes_ref, sinks_ref, q_ref, k_ref, v_ref, kh_ref, vh_ref, qw_ref, kw_ref, *rest,
                 nq, shared_kv, head0, step, max_dist, with_lse, rchunk):
    if with_lse:
        o_ref, lse_ref, qs, ks, vs, bias_s = rest
    else:
        o_ref, qs, ks, vs, bias_s = rest
        lse_ref = None
    u = pl.program_id(1)
    t = pl.program_id(3)
    tq = q_ref.shape[1]
    nkv = 1 if shared_kv else nq

    qi = lax.broadcasted_iota(jnp.int32, (BLOCK, 2 * BLOCK), 0)
    si = lax.broadcasted_iota(jnp.int32, (BLOCK, 2 * BLOCK), 1)
    dist = BLOCK + qi - si
    valid = (dist >= 0) & (dist <= max_dist)
    valid0 = valid & (si >= BLOCK)
    distf = (step * dist).astype(F32)
    for j in range(nq):
        b = -slopes_ref[head0 + u * nq + j] * distf
        bias_s[0, j] = jnp.where(valid0, b, NEG)
        bias_s[1, j] = jnp.where(valid, b, NEG)

    def norm_rows(x, w):
        ms = jnp.mean(x * x, axis=-1, keepdims=True)
        return x * lax.rsqrt(ms + EPS) * w

    def pick_kv(x):
        return jnp.where(u == 0, x[:, :HEAD_DIM], x[:, HEAD_DIM:])

    qw = qw_ref[...]
    kw = kw_ref[...]

    def prep(c, _):
        r0 = pl.multiple_of(c * rchunk, rchunk)
        xq = q_ref[0, pl.ds(r0, rchunk), :]
        for j in range(nq):
            qn = norm_rows(xq[:, j * HEAD_DIM:(j + 1) * HEAD_DIM], qw) * (HEAD_DIM ** -0.5)
            qs[j, pl.ds(r0, rchunk), :] = qn.astype(BF16)
        xk = k_ref[0, pl.ds(r0, rchunk), :]
        xv = v_ref[0, pl.ds(r0, rchunk), :]
        r1 = pl.multiple_of(BLOCK + c * rchunk, BLOCK)
        if shared_kv:
            ks[0, pl.ds(r1, rchunk), :] = norm_rows(pick_kv(xk), kw).astype(BF16)
            vs[0, pl.ds(r1, rchunk), :] = pick_kv(xv).astype(BF16)
        else:
            for j in range(nq):
                ks[j, pl.ds(r1, rchunk), :] = norm_rows(xk[:, j * HEAD_DIM:(j + 1) * HEAD_DIM], kw).astype(BF16)
                vs[j, pl.ds(r1, rchunk), :] = xv[:, j * HEAD_DIM:(j + 1) * HEAD_DIM].astype(BF16)
        return 0

    lax.fori_loop(0, tq // rchunk, prep, 0)
    xk = kh_ref[0]
    xv = vh_ref[0]
    if shared_kv:
        ks[0, 0:BLOCK, :] = norm_rows(pick_kv(xk), kw).astype(BF16)
        vs[0, 0:BLOCK, :] = pick_kv(xv).astype(BF16)
    else:
        for j in range(nq):
            ks[j, 0:BLOCK, :] = norm_rows(xk[:, j * HEAD_DIM:(j + 1) * HEAD_DIM], kw).astype(BF16)
            vs[j, 0:BLOCK, :] = xv[:, j * HEAD_DIM:(j + 1) * HEAD_DIM].astype(BF16)

    def blk(n, _):
        r0 = pl.multiple_of(n * BLOCK, BLOCK)
        bi = jnp.where(jnp.logical_and(t == 0, n == 0), 0, 1)
        outs, lses = [], []
        for j in range(nq):
            kvj = 0 if shared_kv else j
            q = qs[j, pl.ds(r0, BLOCK), :]
            kk = ks[kvj, pl.ds(r0, 2 * BLOCK), :]
            vv = vs[kvj, pl.ds(r0, 2 * BLOCK), :]
            s = lax.dot_general(q, kk, (((1,), (1,)), ((), ())), preferred_element_type=F32)
            s = s + bias_s[bi, j]
            m = jnp.max(s, axis=-1, keepdims=True)
            p = jnp.exp(s - m)
            l = jnp.sum(p, axis=-1, keepdims=True)
            acc = jnp.dot(p.astype(BF16), vv, preferred_element_type=F32)
            if with_lse:
                outs.append(acc / l)
                lses.append(jnp.broadcast_to(m + jnp.log(l), (BLOCK, HEAD_DIM)))
            else:
                outs.append(acc / (l + jnp.exp(sinks_ref[u * nq + j] - m)))
        o_ref[0, pl.ds(r0, BLOCK), :] = jnp.concatenate(outs, axis=-1)
        if with_lse:
            lse_ref[0, pl.ds(r0, BLOCK), :] = jnp.concatenate(lses, axis=-1)
        return 0

    lax.fori_loop(0, tq // BLOCK, blk, 0)


def band_attention(qkv, slopes, sinks, qw, kw, *, mixer, dilation=1, window=A_WINDOW):
    bsz, seq, ncols = qkv.shape
    d = dilation
    ls = seq // d
    view = qkv.reshape(bsz, ls, d * ncols)
    cb = ncols // LANE
    tq = min(ls, 2048)
    nt = ls // tq
    hb = tq // BLOCK
    if mixer == "a":
        nq, units, shared = A_Q_HEADS // A_KV_HEADS, A_KV_HEADS, True
        qwid = nq * HEAD_DIM
        qmap = lambda b, u, r, t: (b, t, u)
        kcol, vcol = 4, 5
        kmap = lambda b, u, r, t: (b, t, kcol)
        vmap = lambda b, u, r, t: (b, t, vcol)
        khmap = lambda b, u, r, t: (b, jnp.maximum(t * hb - 1, 0), kcol)
        vhmap = lambda b, u, r, t: (b, jnp.maximum(t * hb - 1, 0), vcol)
        omap = lambda b, u, r, t: (b, t, u)
        head0, step, max_dist, with_lse = 0, 1, window - 1, False
        ocols = A_Q_HEADS * HEAD_DIM
    else:
        nq, units, shared = 2, B_HEADS // 2, False
        qwid = nq * HEAD_DIM
        qmap = lambda b, u, r, t: (b, t, r * cb + 6 + u)
        kmap = lambda b, u, r, t: (b, t, r * cb + 10 + u)
        vmap = lambda b, u, r, t: (b, t, r * cb + 14 + u)
        khmap = lambda b, u, r, t: (b, jnp.maximum(t * hb - 1, 0), r * cb + 10 + u)
        vhmap = lambda b, u, r, t: (b, jnp.maximum(t * hb - 1, 0), r * cb + 14 + u)
        omap = lambda b, u, r, t: (b, t, r * units + u)
        head0, step, max_dist, with_lse = A_Q_HEADS, d, window // d, True
        ocols = B_HEADS * HEAD_DIM
    nkv = 1 if shared else nq
    smem = pl.BlockSpec(memory_space=pltpu.SMEM)
    oshape = jax.ShapeDtypeStruct((bsz, ls, d * ocols), F32)
    ospec = pl.BlockSpec((1, tq, qwid), omap)
    outs = pl.pallas_call(
        functools.partial(_band_kernel, nq=nq, shared_kv=shared, head0=head0, step=step, max_dist=max_dist,
                          with_lse=with_lse, rchunk=256),
        grid=(bsz, units, d, nt),
        in_specs=[smem, smem,
                  pl.BlockSpec((1, tq, qwid), qmap),
                  pl.BlockSpec((1, tq, LANE), kmap),
                  pl.BlockSpec((1, tq, LANE), vmap),
                  pl.BlockSpec((1, BLOCK, LANE), khmap),
                  pl.BlockSpec((1, BLOCK, LANE), vhmap),
                  _resident((1, HEAD_DIM)), _resident((1, HEAD_DIM))],
        out_specs=(ospec, ospec) if with_lse else ospec,
        out_shape=(oshape, oshape) if with_lse else oshape,
        scratch_shapes=[pltpu.VMEM((nq, tq, HEAD_DIM), BF16),
                        pltpu.VMEM((nkv, BLOCK + tq, HEAD_DIM), BF16),
                        pltpu.VMEM((nkv, BLOCK + tq, HEAD_DIM), BF16),
                        pltpu.VMEM((2, nq, BLOCK, 2 * BLOCK), F32)],
        compiler_params=_cparams(("parallel", "parallel", "parallel", "arbitrary")),
        name=f"band_{mixer}{d}",
    )(slopes, sinks, view, view, view, view, view, qw.reshape(1, HEAD_DIM), kw.reshape(1, HEAD_DIM))
    if with_lse:
        return outs[0].reshape(bsz, seq, ocols), outs[1].reshape(bsz, seq, ocols)
    return outs.reshape(bsz, seq, ocols)


def _attn_out_kernel(x_ref, g_ref, oa_ref, o1_ref, o2_ref, o3_ref, l1_ref, l2_ref, l3_ref, w_ref, out_ref):
    l1, l2, l3 = l1_ref[...], l2_ref[...], l3_ref[...]
    m = jnp.maximum(jnp.maximum(l1, l2), l3)
    e1, e2, e3 = jnp.exp(l1 - m), jnp.exp(l2 - m), jnp.exp(l3 - m)
    ob = (e1 * o1_ref[...] + e2 * o2_ref[...] + e3 * o3_ref[...]) / (e1 + e2 + e3)
    half = oa_ref.shape[-1]
    y = jnp.dot(oa_ref[...].astype(BF16), w_ref[:half, :], preferred_element_type=F32)
    y = y + jnp.dot(ob.astype(BF16), w_ref[half:, :], preferred_element_type=F32)
    out_ref[...] = x_ref[...] + g_ref[0] * y


def attn_out(x2, seq, gate, oa, obs, lses, w, tm=512):
    n, d = x2.shape
    per = seq // tm
    half = oa.shape[-1]
    row = lambda i: (i, 0)
    hs = pl.BlockSpec((tm, half), row)
    return pl.pallas_call(
        _attn_out_kernel,
        grid=(n // tm,),
        in_specs=[pl.BlockSpec((tm, d), row), pl.BlockSpec((1, 1, d), lambda i: (i // per, 0, 0)),
                  hs, hs, hs, hs, hs, hs, hs, _resident(w.shape)],
        out_specs=pl.BlockSpec((tm, d), row),
        out_shape=jax.ShapeDtypeStruct((n, d), F32),
        compiler_params=_cparams(("parallel",)),
        name="attn_out",
    )(x2, gate, oa, *obs, *lses, w)


def _gelu_tanh(x):
    return 0.5 * x * (1.0 + jnp.tanh(math.sqrt(2.0 / math.pi) * (x + 0.044715 * (x * x * x))))


def _s5_kernel(u_ref, e_ref, pr_ref, pi_ref, cr_ref, ci_ref, d_ref, gw_ref, gb_ref, o_ref, xr_s, xi_s, car_s):
    t = pl.program_id(1)
    tc = u_ref.shape[1]
    ns = pr_ref.shape[1]

    @pl.when(t == 0)
    def _():
        car_s[...] = jnp.zeros_like(car_s)

    u = u_ref[0]
    e = jnp.dot(u.astype(BF16), e_ref[...], preferred_element_type=F32)
    xr, xi = e[:, :ns], e[:, ns:]
    row = lax.broadcasted_iota(jnp.int32, (tc, ns), 0) % 8
    for s in (1, 2, 4):
        ar, ai = pr_ref[s - 1:s, :], pi_ref[s - 1:s, :]
        keep = row >= s
        sr = jnp.where(keep, pltpu.roll(xr, s, 0), 0.0)
        si = jnp.where(keep, pltpu.roll(xi, s, 0), 0.0)
        xr, xi = xr + (ar * sr - ai * si), xi + (ar * si + ai * sr)
    xr_s[...] = xr
    xi_s[...] = xi
    pr, pi = pr_ref[...], pi_ref[...]

    def grp(k, carry):
        cr, ci = carry
        r0 = pl.multiple_of(k * 8, 8)
        nr = xr_s[pl.ds(r0, 8), :] + (pr * cr - pi * ci)
        ni = xi_s[pl.ds(r0, 8), :] + (pr * ci + pi * cr)
        xr_s[pl.ds(r0, 8), :] = nr
        xi_s[pl.ds(r0, 8), :] = ni
        return nr[7:8, :], ni[7:8, :]

    cr, ci = lax.fori_loop(0, tc // 8, grp, (car_s[0:1, :], car_s[1:2, :]))
    car_s[0:1, :] = cr
    car_s[1:2, :] = ci
    y = (jnp.dot(xr_s[...].astype(BF16), cr_ref[...], preferred_element_type=F32)
         - jnp.dot(xi_s[...].astype(BF16), ci_ref[...], preferred_element_type=F32)
         + d_ref[...] * u)
    g = _gelu_tanh(y)
    z = jnp.dot(g.astype(BF16), gw_ref[...], preferred_element_type=F32) + gb_ref[...]
    o_ref[0] = g * _sigmoid(z)


def s5_mixer(rec, lam_re, lam_im, log_dt, b_re, b_im, c_re, c_im, d_skip, glu_w, glu_b, tc=512):
    bsz, seq, _ = rec.shape
    ns = S5_GROUPS * S5_STATE
    dt = jnp.exp(log_dt)[:, None]
    lr, li = lam_re, lam_im
    mag, ang = jnp.exp(lr * dt), li * dt
    ab_re, ab_im = mag * jnp.cos(ang), mag * jnp.sin(ang)
    nr, ni = ab_re - 1.0, ab_im
    den = lr * lr + li * li
    f_re = (nr * lr + ni * li) / den
    f_im = (ni * lr - nr * li) / den
    kk = jnp.arange(1, 9, dtype=F32)[:, None, None]
    pmag = jnp.exp(kk * (lr * dt)[None])
    p_re = (pmag * jnp.cos(kk * ang[None])).reshape(8, ns)
    p_im = (pmag * jnp.sin(kk * ang[None])).reshape(8, ns)
    e_re = f_re[..., None] * b_re - f_im[..., None] * b_im
    e_im = f_re[..., None] * b_im + f_im[..., None] * b_re
    eye = jnp.eye(S5_GROUPS, dtype=F32)

    def bd_in(m):
        return jnp.einsum("gpi,gh->gihp", m, eye).reshape(S5_WIDTH, ns)

    def bd_out(m):
        return jnp.einsum("gip,gh->gphi", m, eye).reshape(ns, S5_WIDTH)

    e_mat = jnp.concatenate([bd_in(e_re), bd_in(e_im)], axis=1).astype(BF16)
    return pl.pallas_call(
        _s5_kernel,
        grid=(bsz, seq // tc),
        in_specs=[pl.BlockSpec((1, tc, S5_WIDTH), lambda b, t: (b, t, 0)),
                  _resident((S5_WIDTH, 2 * ns)), _resident((8, ns)), _resident((8, ns)),
                  _resident((ns, S5_WIDTH)), _resident((ns, S5_WIDTH)),
                  _resident((1, S5_WIDTH)), _resident((S5_WIDTH, S5_WIDTH)), _resident((1, S5_WIDTH))],
        out_specs=pl.BlockSpec((1, tc, S5_WIDTH), lambda b, t: (b, t, 0)),
        out_shape=jax.ShapeDtypeStruct((bsz, seq, S5_WIDTH), F32),
        scratch_shapes=[pltpu.VMEM((tc, ns), F32), pltpu.VMEM((tc, ns), F32), pltpu.VMEM((8, ns), F32)],
        compiler_params=_cparams(("parallel", "arbitrary")),
        name="s5",
    )(rec, e_mat, p_re, p_im, bd_out(c_re).astype(BF16), bd_out(c_im).astype(BF16),
      d_skip.reshape(1, S5_WIDTH), glu_w.astype(BF16), glu_b.reshape(1, S5_WIDTH))


def _dot_bf(a, b):
    return jnp.dot(a.astype(BF16), b.astype(BF16), preferred_element_type=F32)


def _dot_nt_bf(a, b):
    return lax.dot_general(a.astype(BF16), b.astype(BF16), (((1,), (1,)), ((), ())), preferred_element_type=F32)


def _dot_tn_bf(a, b):
    return lax.dot_general(a.astype(BF16), b.astype(BF16), (((0,), (0,)), ((), ())), preferred_element_type=F32)


def _dot_01(m01, x):
    hi = x.astype(BF16)
    r = x - hi.astype(F32)
    mid = r.astype(BF16)
    lo = (r - mid.astype(F32)).astype(BF16)
    return (jnp.dot(m01, hi, preferred_element_type=F32) + jnp.dot(m01, mid, preferred_element_type=F32)
            + jnp.dot(m01, lo, preferred_element_type=F32))


def _dn_kernel(alog_ref, dtb_ref, q_ref, k_ref, v_ref, qh_ref, kh_ref, vh_ref, z_ref, ab_ref,
               wq_ref, wk_ref, wv_ref, nw_ref, o_ref, s_s):
    hp = pl.program_id(1)
    t = pl.program_id(2)
    tc = q_ref.shape[1]
    c = DN_CHUNK

    @pl.when(t == 0)
    def _():
        s_s[...] = jnp.zeros_like(s_s)

    live = (t > 0).astype(F32)

    def conv_silu(x_ref, halo_ref, w_ref):
        xe = jnp.concatenate([halo_ref[0] * live, x_ref[0]], axis=0)
        w = w_ref[...]
        y = w[DN_CONV - 1:DN_CONV, :] * xe[8:, :]
        for s in range(1, DN_CONV):
            y = y + w[DN_CONV - 1 - s:DN_CONV - s, :] * pltpu.roll(xe, s, 0)[8:, :]
        return _silu(y)

    q2 = conv_silu(q_ref, qh_ref, wq_ref)
    k2 = conv_silu(k_ref, kh_ref, wk_ref)
    v2 = conv_silu(v_ref, vh_ref, wv_ref)
    ab = ab_ref[0]
    lane = lax.broadcasted_iota(jnp.int32, ab.shape, 1)

    ri = lax.broadcasted_iota(jnp.int32, (c, c), 0)
    ci = lax.broadcasted_iota(jnp.int32, (c, c), 1)
    causal = ri >= ci
    strict = ri > ci
    tril = causal.astype(BF16)
    triu = (ri <= ci).astype(F32)
    ones = jnp.ones((c, c), BF16)

    nck = tc // c
    qn, kn, betas, gs = [], [], [], []
    for j in range(DN_HPS):
        h = hp * DN_HPS + j
        hl = slice(j * DN_DK, (j + 1) * DN_DK)
        q, k = q2[:, hl], k2[:, hl]
        qn.append(q * lax.rsqrt(jnp.sum(q * q, axis=-1, keepdims=True) + EPS) * (DN_DK ** -0.5))
        kn.append(k * lax.rsqrt(jnp.sum(k * k, axis=-1, keepdims=True) + EPS))
        a = jnp.sum(jnp.where(lane == h, ab, 0.0), axis=-1, keepdims=True)
        braw = jnp.sum(jnp.where(lane == DN_HEADS + h, ab, 0.0), axis=-1, keepdims=True)
        betas.append(_sigmoid(braw))
        xa = a + dtb_ref[h]
        softplus = jnp.maximum(xa, 0.0) + jnp.log(1.0 + jnp.exp(-jnp.abs(xa)))
        gs.append(-jnp.exp(alog_ref[h]) * softplus)

    probs = [(i, j) for i in range(nck) for j in range(DN_HPS)]
    rows = lambda x, i: x[i * c:(i + 1) * c]
    qc = [rows(qn[j], i) for i, j in probs]
    kc = [rows(kn[j], i) for i, j in probs]
    vc = [rows(v2[:, j * DN_DK:(j + 1) * DN_DK], i) for i, j in probs]
    bc = [rows(betas[j], i) for i, j in probs]
    gb = [jnp.broadcast_to(rows(gs[j], i), (c, c)) for i, j in probs]
    cs_l = jnp.concatenate([tril, -ones], axis=1)
    zero = jnp.zeros((c, c), F32)
    cs = [_dot_01(cs_l, jnp.concatenate([jnp.concatenate([x, x], axis=1),
                                         jnp.concatenate([x * triu, zero], axis=1)], axis=0)) for x in gb]
    gdiff = [x[:, :c] for x in cs]
    gi = [x[:, c:c + 1] for x in cs]
    glast = [x[c - 1:c, c:c + 1] for x in cs]
    qkk = [_dot_nt_bf(jnp.concatenate([x, y], axis=0), y) for x, y in zip(qc, kc)]
    gamma = [jnp.where(causal, jnp.exp(jnp.where(causal, x, 0.0)), 0.0) for x in gdiff]
    nm = [jnp.where(strict, b * x[c:] * gm, 0.0) for b, x, gm in zip(bc, qkk, gamma)]
    qk = [x[:c] for x in qkk]
    eye = (ri == ci).astype(F32)
    inv = [eye - x for x in nm]
    pw = [_dot_bf(x, x) for x in nm]
    span = 2
    while span < c:
        inv = [x + _dot_bf(x, y) for x, y in zip(inv, pw)]
        span *= 2
        if span < c:
            pw = [_dot_bf(y, y) for y in pw]
    eg = [jnp.exp(x) for x in gi]
    rhs = [jnp.concatenate([v * b, k * (b * e)], axis=-1) for v, k, b, e in zip(vc, kc, bc, eg)]
    sol = [_dot_bf(x, y) for x, y in zip(inv, rhs)]
    qk = [x * gm for x, gm in zip(qk, gamma)]
    kd = [k * jnp.exp(gl - g) for k, gl, g in zip(kc, glast, gi)]
    ks = [_dot_tn_bf(x, y) for x, y in zip(kd, sol)]
    qs = [_dot_bf(x, y) for x, y in zip(qk, sol)]
    lhs = [jnp.concatenate([q * e - x[:, DN_DK:], -y[:, DN_DK:]], axis=0).astype(BF16)
           for q, e, x, y in zip(qc, eg, qs, ks)]
    pre = {p: (lhs[n], qs[n][:, :DN_DK], ks[n][:, :DN_DK], jnp.exp(glast[n])) for n, p in enumerate(probs)}

    nw = nw_ref[...]
    states = [s_s[j] for j in range(DN_HPS)]
    for i in range(nck):
        sl = slice(i * c, (i + 1) * c)
        outs = []
        for j in range(DN_HPS):
            lhs, o0, s0, decay = pre[(i, j)]
            prod = jnp.dot(lhs, states[j].astype(BF16), preferred_element_type=F32)
            o = prod[:c] + o0
            states[j] = states[j] * decay + prod[c:] + s0
            outs.append(o * lax.rsqrt(jnp.mean(o * o, axis=-1, keepdims=True) + EPS) * nw)
        o_ref[0, sl, :] = jnp.concatenate(outs, axis=-1) * _silu(z_ref[0, sl, :])
    for j in range(DN_HPS):
        s_s[j] = states[j]


def deltanet_mixer(rec, conv_w, a_log, dt_bias, out_norm, tc=512):
    bsz, seq, _ = rec.shape
    hb = tc // 8
    wid = DN_HPS * DN_DK
    ngrp = DN_HEADS // DN_HPS
    smem = pl.BlockSpec(memory_space=pltpu.SMEM)

    def col(c0):
        return pl.BlockSpec((1, tc, wid), lambda b, h, t: (b, t, c0 // wid + h))

    def halo(c0):
        return pl.BlockSpec((1, 8, wid), lambda b, h, t: (b, jnp.maximum(t * hb - 1, 0), c0 // wid + h))

    def wcol(c0):
        return pl.BlockSpec((DN_CONV, wid), lambda b, h, t: (0, c0 // wid + h))

    dq = DN_HEADS * DN_DK
    cq, ck, cv, cz = S5_WIDTH, S5_WIDTH + dq, S5_WIDTH + 2 * dq, S5_WIDTH + 3 * dq
    return pl.pallas_call(
        _dn_kernel,
        grid=(bsz, ngrp, seq // tc),
        in_specs=[smem, smem, col(cq), col(ck), col(cv), halo(cq), halo(ck), halo(cv), col(cz),
                  pl.BlockSpec((1, tc, LANE), lambda b, h, t: (b, t, (S5_WIDTH + 4 * dq) // LANE)),
                  wcol(0), wcol(dq), wcol(2 * dq), _resident((1, DN_DK))],
        out_specs=pl.BlockSpec((1, tc, wid), lambda b, h, t: (b, t, h)),
        out_shape=jax.ShapeDtypeStruct((bsz, seq, dq), F32),
        scratch_shapes=[pltpu.VMEM((DN_HPS, DN_DK, DN_DK), F32)],
        compiler_params=_cparams(("parallel", "parallel", "arbitrary")),
        name="deltanet",
    )(a_log, dt_bias, rec, rec, rec, rec, rec, rec, rec, rec, conv_w, conv_w, conv_w, out_norm.reshape(1, DN_DK))


def _rec_out_kernel(x_ref, g_ref, yc_ref, yd_ref, w_ref, out_ref):
    nc = yc_ref.shape[-1]
    y = jnp.dot(yc_ref[...].astype(BF16), w_ref[:nc, :], preferred_element_type=F32)
    y = y + jnp.dot(yd_ref[...].astype(BF16), w_ref[nc:, :], preferred_element_type=F32)
    out_ref[...] = x_ref[...] + g_ref[0] * y


def rec_out(x2, seq, gate, yc, yd, w, tm=512):
    n, d = x2.shape
    per = seq // tm
    row = lambda i: (i, 0)
    return pl.pallas_call(
        _rec_out_kernel,
        grid=(n // tm,),
        in_specs=[pl.BlockSpec((tm, d), row), pl.BlockSpec((1, 1, d), lambda i: (i // per, 0, 0)),
                  pl.BlockSpec((tm, yc.shape[-1]), row), pl.BlockSpec((tm, yd.shape[-1]), row), _resident(w.shape)],
        out_specs=pl.BlockSpec((tm, d), row),
        out_shape=jax.ShapeDtypeStruct((n, d), F32),
        compiler_params=_cparams(("parallel",)),
        name="rec_out",
    )(x2, gate, yc, yd, w)


FFN_HALO = 16


def _ffn_kernel(x_ref, xh_ref, nw_ref, sh_ref, sc_ref, g_ref, wu_ref, cw_ref, wd_ref, out_ref, hb_s, *, per):
    i = pl.program_id(0)
    tm = x_ref.shape[0]
    nf = wd_ref.shape[0]
    nw, sh, sc = nw_ref[...], sh_ref[0], sc_ref[0]
    x = x_ref[...]
    live = (i % per > 0).astype(F32)
    hb_s[0:FFN_HALO, :] = (_modulate(xh_ref[...], nw, sc, sh) * live).astype(BF16)
    hb_s[FFN_HALO:, :] = _modulate(x, nw, sc, sh).astype(BF16)
    hb = hb_s[...]

    def conv(up, w):
        y = w[FFN_CONV - 1:FFN_CONV, :] * up[FFN_HALO:, :]
        for s in range(1, FFN_CONV):
            y = y + w[FFN_CONV - 1 - s:FFN_CONV - s, :] * pltpu.roll(up, s, 0)[FFN_HALO:, :]
        return y

    acc = jnp.zeros((tm, out_ref.shape[-1]), F32)
    for f in range(nf):
        ua = jnp.dot(hb, wu_ref[0, f], preferred_element_type=F32)
        ub = jnp.dot(hb, wu_ref[1, f], preferred_element_type=F32)
        act = _silu(conv(ua, cw_ref[0, f])) * conv(ub, cw_ref[1, f])
        acc = acc + jnp.dot(act.astype(BF16), wd_ref[f], preferred_element_type=F32)
    out_ref[...] = x + g_ref[0] * acc


def conv_ffn(x2, seq, nw, shift, scale, gate, w_up, conv_w, w_down, tm=512, cw=256):
    n, d = x2.shape
    dff = w_down.shape[0]
    nf = dff // cw
    per = seq // tm
    wu = w_up.astype(BF16).reshape(d, 2, nf, cw).transpose(1, 2, 0, 3)
    cwt = conv_w.reshape(FFN_CONV, 2, nf, cw).transpose(1, 2, 0, 3)
    wd = w_down.astype(BF16).reshape(nf, cw, d)
    row = lambda i: (i, 0)
    bat = lambda i: (i // per, 0, 0)
    hpt = tm // FFN_HALO
    return pl.pallas_call(
        functools.partial(_ffn_kernel, per=per),
        grid=(n // tm,),
        in_specs=[pl.BlockSpec((tm, d), row),
                  pl.BlockSpec((FFN_HALO, d), lambda i: (jnp.maximum(i * hpt - 1, 0), 0)),
                  _resident((1, d)), pl.BlockSpec((1, 1, d), bat), pl.BlockSpec((1, 1, d), bat),
                  pl.BlockSpec((1, 1, d), bat),
                  _resident(wu.shape), _resident(cwt.shape), _resident(wd.shape)],
        out_specs=pl.BlockSpec((tm, d), row),
        out_shape=jax.ShapeDtypeStruct((n, d), F32),
        scratch_shapes=[pltpu.VMEM((FFN_HALO + tm, d), BF16)],
        compiler_params=_cparams(("parallel",)),
        name="ffn",
    )(x2, x2, nw.reshape(1, d), shift, scale, gate, wu, cwt, wd)


def _alibi_slopes(n):
    return jnp.asarray(2.0 ** (-8.0 * np.arange(1, n + 1) / n), dtype=F32)


def _pad_cols(w, mult):
    pad = (-w.shape[1]) % mult
    return jnp.pad(w, ((0, 0), (0, pad)))


def kernel(x, c, ada_w, ada_b, norm_mix, norm_ffn, attn_w_in, attn_q_norm_a, attn_k_norm_a, attn_q_norm_b,
           attn_k_norm_b, attn_sinks, attn_w_out, rec_w_in, s5_lambda_re, s5_lambda_im, s5_log_dt, s5_b_re,
           s5_b_im, s5_c_re, s5_c_im, s5_d, s5_glu_w, s5_glu_b, dn_conv, dn_a_log, dn_dt_bias, dn_out_norm,
           rec_w_out, ffn_w_up, ffn_conv, ffn_w_down):
    bsz, seq, d = x.shape
    depth = ada_w.shape[0]
    mod = adaln(c, ada_w, ada_b)
    x2 = x.reshape(bsz * seq, d)
    slopes = _alibi_slopes(N_ATTN_HEADS)
    for layer in range(depth):
        sh1, sc1, g1, sh2, sc2, g2 = [mod[layer, :, j * d:(j + 1) * d].reshape(bsz, 1, d) for j in range(6)]
        i = layer // 2
        if layer % 2 == 0:
            qkv = inproj(x2, seq, norm_mix[layer], sh1, sc1, attn_w_in[i].astype(BF16))
            qkv = qkv.reshape(bsz, seq, -1)
            oa = band_attention(qkv, slopes, attn_sinks[i], attn_q_norm_a[i], attn_k_norm_a[i], mixer="a")
            obs, lses = [], []
            for window, dilation in B_BRANCHES:
                o, l = band_attention(qkv, slopes, attn_sinks[i], attn_q_norm_b[i], attn_k_norm_b[i],
                                      mixer="b", dilation=dilation, window=window)
                obs.append(o.reshape(bsz * seq, -1))
                lses.append(l.reshape(bsz * seq, -1))
            x2 = attn_out(x2, seq, g1, oa.reshape(bsz * seq, -1), obs, lses, attn_w_out[i].astype(BF16))
        else:
            rec = inproj(x2, seq, norm_mix[layer], sh1, sc1, _pad_cols(rec_w_in[i], LANE).astype(BF16))
            rec = rec.reshape(bsz, seq, -1)
            yc = s5_mixer(rec, s5_lambda_re[i], s5_lambda_im[i], s5_log_dt[i], s5_b_re[i], s5_b_im[i],
                          s5_c_re[i], s5_c_im[i], s5_d[i], s5_glu_w[i], s5_glu_b[i])
            yd = deltanet_mixer(rec, dn_conv[i], dn_a_log[i], dn_dt_bias[i], dn_out_norm[i])
            x2 = rec_out(x2, seq, g1, yc.reshape(bsz * seq, -1), yd.reshape(bsz * seq, -1),
                         rec_w_out[i].astype(BF16))
        x2 = conv_ffn(x2, seq, norm_ffn[layer], sh2, sc2, g2, ffn_w_up[layer], ffn_conv[layer], ffn_w_down[layer])
    return x2.reshape(bsz, seq, d)
```

```python
import functools
import math

import numpy as np
import jax
import jax.numpy as jnp
from jax import lax
from jax.experimental import pallas as pl
from jax.experimental.pallas import tpu as pltpu

F32 = jnp.float32
BF16 = jnp.bfloat16
HIGHEST = lax.Precision.HIGHEST
EPS = 1e-6
NEG = -1e30

HEAD_DIM = 64
BLOCK = 128
A_Q_HEADS = 8
A_KV_HEADS = 2
A_WINDOW = 128
B_HEADS = 8
B_BRANCHES = ((128, 1), (512, 4), (2048, 16))
N_ATTN_HEADS = A_Q_HEADS + B_HEADS
S5_GROUP = 16
S5_GROUPS = 16
S5_WIDTH = 256
S5_STATE = 64
DN_HEADS = 6
DN_DK = 128
DN_CONV = 4
DN_CHUNK = 64
DN_HPS = 6
FFN_CONV = 3

V7X_VMEM_LIMIT = 60 * 1024 * 1024
LANE = 128


def _cparams(sem, vmem=V7X_VMEM_LIMIT):
    return pltpu.CompilerParams(dimension_semantics=sem, vmem_limit_bytes=vmem)


def _resident(shape):
    n = len(shape)
    return pl.BlockSpec(shape, lambda *_: (0,) * n, pipeline_mode=pl.Buffered(1))


def _sigmoid(x):
    return 1.0 / (1.0 + jnp.exp(-x))


def _silu(x):
    return x * _sigmoid(x)


def _modulate(x, nw, scale, shift):
    ms = jnp.mean(x * x, axis=-1, keepdims=True)
    y = x * lax.rsqrt(ms + EPS) * nw
    return y * (1.0 + scale) + shift


def _adaln_kernel(c_ref, w_ref, b_ref, o_ref):
    c = c_ref[...]
    o_ref[0] = jnp.dot(_silu(c), w_ref[0], precision=HIGHEST, preferred_element_type=F32) + b_ref[0]


def adaln(c, ada_w, ada_b, tn=1536):
    depth, d, n = ada_w.shape
    bsz = c.shape[0]
    cp = jnp.zeros((8, d), F32).at[:bsz].set(c)
    out = pl.pallas_call(
        _adaln_kernel,
        grid=(depth, n // tn),
        in_specs=[pl.BlockSpec((8, d), lambda l, j: (0, 0)),
                  pl.BlockSpec((1, d, tn), lambda l, j: (l, 0, j)),
                  pl.BlockSpec((1, 1, tn), lambda l, j: (l, 0, j))],
        out_specs=pl.BlockSpec((1, 8, tn), lambda l, j: (l, 0, j)),
        out_shape=jax.ShapeDtypeStruct((depth, 8, n), F32),
        compiler_params=_cparams(("arbitrary", "arbitrary")),
        name="adaln",
    )(cp, ada_w, ada_b.reshape(depth, 1, n))
    return out[:, :bsz]


def _inproj_kernel(x_ref, nw_ref, sh_ref, sc_ref, w_ref, o_ref, *, ncol):
    h = _modulate(x_ref[...], nw_ref[...], sc_ref[0], sh_ref[0]).astype(BF16)
    nout = o_ref.shape[-1]
    for j in range(nout // ncol):
        o_ref[:, j * ncol:(j + 1) * ncol] = jnp.dot(h, w_ref[:, j * ncol:(j + 1) * ncol],
                                                    preferred_element_type=F32)


def inproj(x2, seq, nw, shift, scale, w, tm=512, ncol=1152):
    n, d = x2.shape
    nout = w.shape[1]
    per = seq // tm
    return pl.pallas_call(
        functools.partial(_inproj_kernel, ncol=ncol),
        grid=(n // tm,),
        in_specs=[pl.BlockSpec((tm, d), lambda i: (i, 0)),
                  _resident((1, d)),
                  pl.BlockSpec((1, 1, d), lambda i: (i // per, 0, 0)),
                  pl.BlockSpec((1, 1, d), lambda i: (i // per, 0, 0)),
                  _resident((d, nout))],
        out_specs=pl.BlockSpec((tm, nout), lambda i: (i, 0)),
        out_shape=jax.ShapeDtypeStruct((n, nout), F32),
        compiler_params=_cparams(("parallel",)),
        name="inproj",
    )(x2, nw.reshape(1, d), shift, scale, w)


ATT_SPAN = 2048
ATT_GROUP = 4
ATT_PREP = 256


def _static_or_multiple(x, k):
    return x if isinstance(x, int) else pl.multiple_of(x, k)


def _attn_kernel(slopes_ref, sinks_ref, q_ref, k_ref, v_ref, qw_ref, kw_ref, o_ref, *scratch,
                 branches, head0, with_sinks, dup_kv):
    p = pl.program_id(1)
    t = pl.program_id(2)
    span = q_ref.shape[1]
    nbr = len(branches)
    assert branches[0][0] == 1
    need32 = nbr > 1
    it = iter(scratch)
    qn_s, kn_s = (next(it), next(it)) if need32 else (None, None)
    per = [(next(it), next(it), next(it), next(it)) for _ in branches]
    acc_s, m_s, l_s = (next(it), next(it), next(it)) if nbr > 1 else (None, None, None)

    lane = lax.broadcasted_iota(jnp.int32, (1, LANE), 1)
    low = lane < HEAD_DIM
    li = lax.broadcasted_iota(jnp.int32, (LANE, LANE), 0) // HEAD_DIM
    lj = lax.broadcasted_iota(jnp.int32, (LANE, LANE), 1) // HEAD_DIM
    same_head = (li == lj).astype(BF16)

    for (d, _), (_, ks, vs, _) in zip(branches, per):
        n_d = span // d

        @pl.when(t == 0)
        def _():
            for r in range(d):
                ks[r, 0:BLOCK, :] = jnp.zeros((BLOCK, LANE), BF16)
                for j in range(2):
                    vs[j, r, 0:BLOCK, :] = jnp.zeros((BLOCK, LANE), BF16)

        @pl.when(t > 0)
        def _():
            for r in range(d):
                ks[r, 0:BLOCK, :] = ks[r, n_d:n_d + BLOCK, :]
                for j in range(2):
                    vs[j, r, 0:BLOCK, :] = vs[j, r, n_d:n_d + BLOCK, :]

    qi = lax.broadcasted_iota(jnp.int32, (BLOCK, 2 * BLOCK), 0)
    si = lax.broadcasted_iota(jnp.int32, (BLOCK, 2 * BLOCK), 1)
    dist = BLOCK + qi - si
    for (d, max_dist), (_, _, _, bias) in zip(branches, per):
        valid = (dist >= 0) & (dist <= max_dist)
        valid0 = valid & (si >= BLOCK)
        distf = (d * dist).astype(F32)
        for j in range(2):
            b = -slopes_ref[head0 + 2 * p + j] * distf
            bias[0, j] = jnp.where(valid0, b, NEG)
            bias[1, j] = jnp.where(valid, b, NEG)

    def head_norm(x, w):
        x2 = x * x
        hi = x2.astype(BF16)
        lo = (x2 - hi.astype(F32)).astype(BF16)
        ss = jnp.dot(hi, same_head, preferred_element_type=F32) + jnp.dot(lo, same_head, preferred_element_type=F32)
        return x * lax.rsqrt(ss * (1.0 / HEAD_DIM) + EPS) * w

    def both_halves(x):
        swapped = pltpu.roll(x, HEAD_DIM, 1)
        return jnp.where(lane // HEAD_DIM == p // 2, x, swapped)

    def put_q(qs, rows, x):
        qs[0, rows, :] = jnp.where(low, x, 0.0).astype(BF16)
        qs[1, rows, :] = jnp.where(low, 0.0, x).astype(BF16)

    def put_v(vs, r, rows, x):
        vs[0, r, rows, :] = jnp.where(low, x, 1.0).astype(BF16)
        vs[1, r, rows, :] = jnp.where(low, 1.0, x).astype(BF16)

    qw, kw = qw_ref[...], kw_ref[...]
    qs1, ks1, vs1, _ = per[0]

    def prep(c, _):
        r0 = pl.multiple_of(c * ATT_PREP, ATT_PREP)
        rows = pl.ds(r0, ATT_PREP)
        krows = pl.ds(pl.multiple_of(BLOCK + r0, BLOCK), ATT_PREP)
        qn = head_norm(q_ref[0, rows, :], qw) * (HEAD_DIM ** -0.5)
        xk, xv = k_ref[0, rows, :], v_ref[0, rows, :]
        if dup_kv:
            xk, xv = both_halves(xk), both_halves(xv)
        kn = head_norm(xk, kw)
        if need32:
            qn_s[rows, :] = qn
            kn_s[rows, :] = kn
        put_q(qs1, rows, qn)
        ks1[0, krows, :] = kn.astype(BF16)
        put_v(vs1, 0, krows, xv)
        return 0

    lax.fori_loop(0, span // ATT_PREP, prep, 0)
    for (d, _), (qs, ks, vs, _) in zip(branches[1:], per[1:]):
        n_d = span // d
        for r in range(d):
            sub = pl.ds(r, n_d, stride=d)
            put_q(qs, slice(r * n_d, (r + 1) * n_d), qn_s[sub, :])
            ks[r, BLOCK:BLOCK + n_d, :] = kn_s[sub, :].astype(BF16)
            put_v(vs, r, slice(BLOCK, BLOCK + n_d), v_ref[0, sub, :])

    def process(bi, combos):
        d, _ = branches[bi]
        qs, ks, vs, bias = per[bi]
        n_d = span // d
        qv, kv, vv, bs, out_rows = [], [], [], [], []
        for r, n in combos:
            qrow = _static_or_multiple(r * n_d + n * BLOCK, BLOCK)
            krow = _static_or_multiple(n * BLOCK, BLOCK)
            bidx = jnp.where(jnp.logical_and(t == 0, n == 0), 0, 1)
            for j in range(2):
                qv.append(qs[j, pl.ds(qrow, BLOCK), :])
                kv.append(ks[r, pl.ds(krow, 2 * BLOCK), :])
                vv.append(vs[j, r, pl.ds(krow, 2 * BLOCK), :])
                bs.append(bias[bidx, j])
            out_rows.append(pl.ds(krow, BLOCK) if d == 1 else pl.ds(r + d * BLOCK * n, BLOCK, stride=d))
        s = [lax.dot_general(q, k, (((1,), (1,)), ((), ())), preferred_element_type=F32) + b
             for q, k, b in zip(qv, kv, bs)]
        m = [jnp.max(x, axis=-1, keepdims=True) for x in s]
        pr = [jnp.exp(x - mm).astype(BF16) for x, mm in zip(s, m)]
        acc = [jnp.dot(x, v, preferred_element_type=F32) for x, v in zip(pr, vv)]
        for ci, rows in enumerate(out_rows):
            a0, a1, m0, m1 = acc[2 * ci], acc[2 * ci + 1], m[2 * ci], m[2 * ci + 1]
            num = jnp.where(low, a0, a1)
            den = pltpu.roll(jnp.where(low, a1, a0), HEAD_DIM, 1)
            mx = jnp.where(low, m0, m1)
            if nbr == 1:
                if with_sinks:
                    sk = jnp.where(low, sinks_ref[2 * p], sinks_ref[2 * p + 1])
                    den = den + jnp.exp(sk - mx)
                o_ref[0, rows, :] = num / den
            elif bi == 0:
                acc_s[rows, :] = num
                l_s[rows, :] = den
                m_s[rows, :] = mx
            else:
                m_old = m_s[rows, :]
                m_new = jnp.maximum(m_old, mx)
                w_old, w_new = jnp.exp(m_old - m_new), jnp.exp(mx - m_new)
                num = w_old * acc_s[rows, :] + w_new * num
                den = w_old * l_s[rows, :] + w_new * den
                if bi == nbr - 1:
                    o_ref[0, rows, :] = num / den
                else:
                    acc_s[rows, :] = num
                    l_s[rows, :] = den
                    m_s[rows, :] = m_new

    for bi, (d, _) in enumerate(branches):
        nblk = span // d // BLOCK

        def group(g, _, bi=bi, nblk=nblk):
            if nblk >= ATT_GROUP:
                base = g * ATT_GROUP
                r = base // nblk
                combos = [(r, base % nblk + i) for i in range(ATT_GROUP)]
            else:
                per_grp = ATT_GROUP // nblk
                combos = [(g * per_grp + i // nblk, i % nblk) for i in range(ATT_GROUP)]
            process(bi, combos)
            return 0

        lax.fori_loop(0, span // BLOCK // ATT_GROUP, group, 0)


def _attention_call(qkv, slopes, sinks, qw, kw, *, branches, qblk, kblk, vblk, shared_kv, head0, with_sinks, name):
    bsz, seq, _ = qkv.shape
    span = min(seq, ATT_SPAN)
    npair = 4
    kmap = (lambda b, p, t: (b, t, kblk)) if shared_kv else (lambda b, p, t: (b, t, kblk + p))
    vmap = (lambda b, p, t: (b, t, vblk)) if shared_kv else (lambda b, p, t: (b, t, vblk + p))
    smem = pl.BlockSpec(memory_space=pltpu.SMEM)
    scratch = []
    if len(branches) > 1:
        scratch += [pltpu.VMEM((span, LANE), F32), pltpu.VMEM((span, LANE), F32)]
    for d, _ in branches:
        n_d = span // d
        scratch += [pltpu.VMEM((2, span, LANE), BF16), pltpu.VMEM((d, BLOCK + n_d, LANE), BF16),
                    pltpu.VMEM((2, d, BLOCK + n_d, LANE), BF16), pltpu.VMEM((2, 2, BLOCK, 2 * BLOCK), F32)]
    if len(branches) > 1:
        scratch += [pltpu.VMEM((span, LANE), F32)] * 3
    tile2 = lambda w: jnp.tile(w.reshape(1, HEAD_DIM), (1, 2))
    return pl.pallas_call(
        functools.partial(_attn_kernel, branches=branches, head0=head0, with_sinks=with_sinks, dup_kv=shared_kv),
        grid=(bsz, npair, seq // span),
        in_specs=[smem, smem,
                  pl.BlockSpec((1, span, LANE), lambda b, p, t: (b, t, qblk + p)),
                  pl.BlockSpec((1, span, LANE), kmap),
                  pl.BlockSpec((1, span, LANE), vmap),
                  _resident((1, LANE)), _resident((1, LANE))],
        out_specs=pl.BlockSpec((1, span, LANE), lambda b, p, t: (b, t, p)),
        out_shape=jax.ShapeDtypeStruct((bsz, seq, npair * LANE), F32),
        scratch_shapes=scratch,
        compiler_params=_cparams(("parallel", "parallel", "arbitrary")),
        name=name,
    )(slopes, sinks, qkv, qkv, qkv, tile2(qw), tile2(kw))


def attention(qkv, slopes, sinks, qw_a, kw_a, qw_b, kw_b):
    oa = _attention_call(qkv, slopes, sinks, qw_a, kw_a, branches=((1, A_WINDOW - 1),), qblk=0, kblk=4, vblk=5,
                         shared_kv=True, head0=0, with_sinks=True, name="attn_a")
    ob = _attention_call(qkv, slopes, sinks, qw_b, kw_b, branches=tuple((d, w // d) for w, d in B_BRANCHES),
                         qblk=6, kblk=10, vblk=14, shared_kv=False, head0=A_Q_HEADS, with_sinks=False, name="attn_b")
    return oa, ob


def _mix_out_kernel(x_ref, g_ref, ya_ref, yb_ref, w_ref, out_ref):
    na = ya_ref.shape[-1]
    y = jnp.dot(ya_ref[...].astype(BF16), w_ref[:na, :], preferred_element_type=F32)
    y = y + jnp.dot(yb_ref[...].astype(BF16), w_ref[na:, :], preferred_element_type=F32)
    out_ref[...] = x_ref[...] + g_ref[0] * y


def mix_out(x2, seq, gate, ya, yb, w, name, tm=512):
    n, d = x2.shape
    per = seq // tm
    row = lambda i: (i, 0)
    return pl.pallas_call(
        _mix_out_kernel,
        grid=(n // tm,),
        in_specs=[pl.BlockSpec((tm, d), row), pl.BlockSpec((1, 1, d), lambda i: (i // per, 0, 0)),
                  pl.BlockSpec((tm, ya.shape[-1]), row), pl.BlockSpec((tm, yb.shape[-1]), row), _resident(w.shape)],
        out_specs=pl.BlockSpec((tm, d), row),
        out_shape=jax.ShapeDtypeStruct((n, d), F32),
        compiler_params=_cparams(("parallel",)),
        name=name,
    )(x2, gate, ya, yb, w)


def _gelu_tanh(x):
    return 0.5 * x * (1.0 + jnp.tanh(math.sqrt(2.0 / math.pi) * (x + 0.044715 * (x * x * x))))


def _s5_kernel(u_ref, e_ref, pr_ref, pi_ref, cr_ref, ci_ref, d_ref, gw_ref, gb_ref, o_ref, xr_s, xi_s, car_s):
    t = pl.program_id(1)
    tc = u_ref.shape[1]
    ns = pr_ref.shape[1]

    @pl.when(t == 0)
    def _():
        car_s[...] = jnp.zeros_like(car_s)

    u = u_ref[0]
    e = jnp.dot(u.astype(BF16), e_ref[...], preferred_element_type=F32)
    xr, xi = e[:, :ns], e[:, ns:]
    row = lax.broadcasted_iota(jnp.int32, (tc, ns), 0) % 8
    for s in (1, 2, 4):
        ar, ai = pr_ref[s - 1:s, :], pi_ref[s - 1:s, :]
        keep = row >= s
        sr = jnp.where(keep, pltpu.roll(xr, s, 0), 0.0)
        si = jnp.where(keep, pltpu.roll(xi, s, 0), 0.0)
        xr, xi = xr + (ar * sr - ai * si), xi + (ar * si + ai * sr)
    xr_s[...] = xr
    xi_s[...] = xi
    pr, pi = pr_ref[...], pi_ref[...]

    def grp(k, carry):
        cr, ci = carry
        r0 = pl.multiple_of(k * 8, 8)
        nr = xr_s[pl.ds(r0, 8), :] + (pr * cr - pi * ci)
        ni = xi_s[pl.ds(r0, 8), :] + (pr * ci + pi * cr)
        xr_s[pl.ds(r0, 8), :] = nr
        xi_s[pl.ds(r0, 8), :] = ni
        return nr[7:8, :], ni[7:8, :]

    cr, ci = lax.fori_loop(0, tc // 8, grp, (car_s[0:1, :], car_s[1:2, :]))
    car_s[0:1, :] = cr
    car_s[1:2, :] = ci
    y = (jnp.dot(xr_s[...].astype(BF16), cr_ref[...], preferred_element_type=F32)
         - jnp.dot(xi_s[...].astype(BF16), ci_ref[...], preferred_element_type=F32)
         + d_ref[...] * u)
    g = _gelu_tanh(y)
    z = jnp.dot(g.astype(BF16), gw_ref[...], preferred_element_type=F32) + gb_ref[...]
    o_ref[0] = g * _sigmoid(z)


def s5_mixer(rec, lam_re, lam_im, log_dt, b_re, b_im, c_re, c_im, d_skip, glu_w, glu_b, ucol=0, tc=512):
    bsz, seq, _ = rec.shape
    ublk = ucol // S5_WIDTH
    ns = S5_GROUPS * S5_STATE
    dt = jnp.exp(log_dt)[:, None]
    lr, li = lam_re, lam_im
    mag, ang = jnp.exp(lr * dt), li * dt
    ab_re, ab_im = mag * jnp.cos(ang), mag * jnp.sin(ang)
    nr, ni = ab_re - 1.0, ab_im
    den = lr * lr + li * li
    f_re = (nr * lr + ni * li) / den
    f_im = (ni * lr - nr * li) / den
    kk = jnp.arange(1, 9, dtype=F32)[:, None, None]
    pmag = jnp.exp(kk * (lr * dt)[None])
    p_re = (pmag * jnp.cos(kk * ang[None])).reshape(8, ns)
    p_im = (pmag * jnp.sin(kk * ang[None])).reshape(8, ns)
    e_re = f_re[..., None] * b_re - f_im[..., None] * b_im
    e_im = f_re[..., None] * b_im + f_im[..., None] * b_re
    eye = jnp.eye(S5_GROUPS, dtype=F32)

    def bd_in(m):
        return jnp.einsum("gpi,gh->gihp", m, eye).reshape(S5_WIDTH, ns)

    def bd_out(m):
        return jnp.einsum("gip,gh->gphi", m, eye).reshape(ns, S5_WIDTH)

    e_mat = jnp.concatenate([bd_in(e_re), bd_in(e_im)], axis=1).astype(BF16)
    return pl.pallas_call(
        _s5_kernel,
        grid=(bsz, seq // tc),
        in_specs=[pl.BlockSpec((1, tc, S5_WIDTH), lambda b, t: (b, t, ublk)),
                  _resident((S5_WIDTH, 2 * ns)), _resident((8, ns)), _resident((8, ns)),
                  _resident((ns, S5_WIDTH)), _resident((ns, S5_WIDTH)),
                  _resident((1, S5_WIDTH)), _resident((S5_WIDTH, S5_WIDTH)), _resident((1, S5_WIDTH))],
        out_specs=pl.BlockSpec((1, tc, S5_WIDTH), lambda b, t: (b, t, 0)),
        out_shape=jax.ShapeDtypeStruct((bsz, seq, S5_WIDTH), F32),
        scratch_shapes=[pltpu.VMEM((tc, ns), F32), pltpu.VMEM((tc, ns), F32), pltpu.VMEM((8, ns), F32)],
        compiler_params=_cparams(("parallel", "arbitrary")),
        name="s5",
    )(rec, e_mat, p_re, p_im, bd_out(c_re).astype(BF16), bd_out(c_im).astype(BF16),
      d_skip.reshape(1, S5_WIDTH), glu_w.astype(BF16), glu_b.reshape(1, S5_WIDTH))


def _dot_bf(a, b):
    return jnp.dot(a.astype(BF16), b.astype(BF16), preferred_element_type=F32)


def _dot_nt_bf(a, b):
    return lax.dot_general(a.astype(BF16), b.astype(BF16), (((1,), (1,)), ((), ())), preferred_element_type=F32)


def _dot_tn_bf(a, b):
    return lax.dot_general(a.astype(BF16), b.astype(BF16), (((0,), (0,)), ((), ())), preferred_element_type=F32)


def _split_bf(x):
    hi = x.astype(BF16)
    return hi, (x - hi.astype(F32)).astype(BF16)


def _dot_x3(a, b):
    ah, al = _split_bf(a)
    bh, bl = _split_bf(b)
    return (jnp.dot(ah, bh, preferred_element_type=F32) + jnp.dot(ah, bl, preferred_element_type=F32)
            + jnp.dot(al, bh, preferred_element_type=F32))


def _dot_01(m01, x):
    hi = x.astype(BF16)
    r = x - hi.astype(F32)
    mid = r.astype(BF16)
    lo = (r - mid.astype(F32)).astype(BF16)
    return (jnp.dot(m01, hi, preferred_element_type=F32) + jnp.dot(m01, mid, preferred_element_type=F32)
            + jnp.dot(m01, lo, preferred_element_type=F32))


def _dn_kernel(alog_ref, dtb_ref, q_ref, k_ref, v_ref, qh_ref, kh_ref, vh_ref, z_ref, ab_ref,
               wq_ref, wk_ref, wv_ref, nw_ref, o_ref, s_s):
    hp = pl.program_id(1)
    t = pl.program_id(2)
    tc = q_ref.shape[1]
    c = DN_CHUNK

    @pl.when(t == 0)
    def _():
        s_s[...] = jnp.zeros_like(s_s)

    live = (t > 0).astype(F32)

    def conv_silu(x_ref, halo_ref, w_ref):
        xe = jnp.concatenate([halo_ref[0] * live, x_ref[0]], axis=0)
        w = w_ref[...]
        y = w[DN_CONV - 1:DN_CONV, :] * xe[8:, :]
        for s in range(1, DN_CONV):
            y = y + w[DN_CONV - 1 - s:DN_CONV - s, :] * pltpu.roll(xe, s, 0)[8:, :]
        return _silu(y)

    q2 = conv_silu(q_ref, qh_ref, wq_ref)
    k2 = conv_silu(k_ref, kh_ref, wk_ref)
    v2 = conv_silu(v_ref, vh_ref, wv_ref)
    ab = ab_ref[0]
    lane = lax.broadcasted_iota(jnp.int32, ab.shape, 1)

    ri = lax.broadcasted_iota(jnp.int32, (c, c), 0)
    ci = lax.broadcasted_iota(jnp.int32, (c, c), 1)
    causal = ri >= ci
    strict = ri > ci
    tril = causal.astype(BF16)
    triu = (ri <= ci).astype(F32)
    ones = jnp.ones((c, c), BF16)

    nck = tc // c
    qn, kn, betas, gs = [], [], [], []
    for j in range(DN_HPS):
        h = hp * DN_HPS + j
        hl = slice(j * DN_DK, (j + 1) * DN_DK)
        q, k = q2[:, hl], k2[:, hl]
        qn.append(q * lax.rsqrt(jnp.sum(q * q, axis=-1, keepdims=True) + EPS) * (DN_DK ** -0.5))
        kn.append(k * lax.rsqrt(jnp.sum(k * k, axis=-1, keepdims=True) + EPS))
        a = jnp.sum(jnp.where(lane == h, ab, 0.0), axis=-1, keepdims=True)
        braw = jnp.sum(jnp.where(lane == DN_HEADS + h, ab, 0.0), axis=-1, keepdims=True)
        betas.append(_sigmoid(braw))
        xa = a + dtb_ref[h]
        softplus = jnp.maximum(xa, 0.0) + jnp.log(1.0 + jnp.exp(-jnp.abs(xa)))
        gs.append(-jnp.exp(alog_ref[h]) * softplus)

    probs = [(i, j) for i in range(nck) for j in range(DN_HPS)]
    rows = lambda x, i: x[i * c:(i + 1) * c]
    qc = [rows(qn[j], i) for i, j in probs]
    kc = [rows(kn[j], i) for i, j in probs]
    vc = [rows(v2[:, j * DN_DK:(j + 1) * DN_DK], i) for i, j in probs]
    bc = [rows(betas[j], i) for i, j in probs]
    gb = [jnp.broadcast_to(rows(gs[j], i), (c, c)) for i, j in probs]
    cs_l = jnp.concatenate([tril, -ones], axis=1)
    zero = jnp.zeros((c, c), F32)
    cs = [_dot_01(cs_l, jnp.concatenate([jnp.concatenate([x, x], axis=1),
                                         jnp.concatenate([x * triu, zero], axis=1)], axis=0)) for x in gb]
    gdiff = [x[:, :c] for x in cs]
    gi = [x[:, c:c + 1] for x in cs]
    glast = [x[c - 1:c, c:c + 1] for x in cs]
    qkk = [_dot_nt_bf(jnp.concatenate([x, y], axis=0), y) for x, y in zip(qc, kc)]
    gamma = [jnp.where(causal, jnp.exp(jnp.where(causal, x, 0.0)), 0.0) for x in gdiff]
    nm = [jnp.where(strict, b * x[c:] * gm, 0.0) for b, x, gm in zip(bc, qkk, gamma)]
    qk = [x[:c] for x in qkk]
    eye = (ri == ci).astype(F32)
    inv = [eye - x for x in nm]
    pw = [_dot_x3(x, x) for x in nm]
    span = 2
    while span < c:
        inv = [x + _dot_x3(x, y) for x, y in zip(inv, pw)]
        span *= 2
        if span < c:
            pw = [_dot_x3(y, y) for y in pw]
    eg = [jnp.exp(x) for x in gi]
    qk = [x * gm for x, gm in zip(qk, gamma)]
    kd = [k * jnp.exp(gl - g) for k, gl, g in zip(kc, glast, gi)]
    w2 = [jnp.concatenate([_dot_tn_bf(x, t), _dot_bf(y, t)], axis=0).astype(BF16) for x, y, t in zip(kd, qk, inv)]
    ke = [k * e for k, e in zip(kc, eg)]
    ke_hi = [x.astype(BF16) for x in ke]
    ke_lo = [(x - h.astype(F32)).astype(BF16) for x, h in zip(ke, ke_hi)]
    l1 = [jnp.concatenate([h, lo, (q * e).astype(BF16)], axis=0) for h, lo, q, e in zip(ke_hi, ke_lo, qc, eg)]
    decay = [jnp.exp(x) for x in glast]

    nw = nw_ref[...]
    states = [s_s[j] for j in range(DN_HPS)]
    for i in range(nck):
        sl = slice(i * c, (i + 1) * c)
        ns = [i * DN_HPS + j for j in range(DN_HPS)]
        s_hi = [s.astype(BF16) for s in states]
        s_lo = [(s - h.astype(F32)).astype(BF16) for s, h in zip(states, s_hi)]
        p1 = [jnp.dot(l1[n], h, preferred_element_type=F32) for n, h in zip(ns, s_hi)]
        p2 = [jnp.dot(ke_hi[n], lo, preferred_element_type=F32) for n, lo in zip(ns, s_lo)]
        res = [bc[n] * (vc[n] - (a[:c] + a[c:2 * c] + b)) for n, a, b in zip(ns, p1, p2)]
        p3 = [jnp.dot(w2[n], r.astype(BF16), preferred_element_type=F32) for n, r in zip(ns, res)]
        outs = []
        for j in range(DN_HPS):
            o = p1[j][2 * c:] + p3[j][DN_DK:]
            states[j] = states[j] * decay[ns[j]] + p3[j][:DN_DK]
            outs.append(o * lax.rsqrt(jnp.mean(o * o, axis=-1, keepdims=True) + EPS) * nw)
        o_ref[0, sl, :] = jnp.concatenate(outs, axis=-1) * _silu(z_ref[0, sl, :])
    for j in range(DN_HPS):
        s_s[j] = states[j]


def deltanet_mixer(rec, conv_w, a_log, dt_bias, out_norm, cols, tc=256):
    bsz, seq, _ = rec.shape
    cq, ck, cv, cz, cab = cols
    hb = tc // 8
    wid = DN_HPS * DN_DK
    ngrp = DN_HEADS // DN_HPS
    smem = pl.BlockSpec(memory_space=pltpu.SMEM)

    def col(c0):
        return pl.BlockSpec((1, tc, wid), lambda b, h, t: (b, t, c0 // wid + h))

    def halo(c0):
        return pl.BlockSpec((1, 8, wid), lambda b, h, t: (b, jnp.maximum(t * hb - 1, 0), c0 // wid + h))

    def wcol(c0):
        return pl.BlockSpec((DN_CONV, wid), lambda b, h, t: (0, c0 // wid + h))

    dq = DN_HEADS * DN_DK
    return pl.pallas_call(
        _dn_kernel,
        grid=(bsz, ngrp, seq // tc),
        in_specs=[smem, smem, col(cq), col(ck), col(cv), halo(cq), halo(ck), halo(cv), col(cz),
                  pl.BlockSpec((1, tc, LANE), lambda b, h, t: (b, t, cab // LANE)),
                  wcol(0), wcol(dq), wcol(2 * dq), _resident((1, DN_DK))],
        out_specs=pl.BlockSpec((1, tc, wid), lambda b, h, t: (b, t, h)),
        out_shape=jax.ShapeDtypeStruct((bsz, seq, dq), F32),
        scratch_shapes=[pltpu.VMEM((DN_HPS, DN_DK, DN_DK), F32)],
        compiler_params=_cparams(("parallel", "parallel", "arbitrary")),
        name="deltanet",
    )(a_log, dt_bias, rec, rec, rec, rec, rec, rec, rec, rec, conv_w, conv_w, conv_w, out_norm.reshape(1, DN_DK))


FFN_HALO = 16
FFN_AHEAD = 3


def _ffn_kernel(x_ref, xh_ref, nw_ref, sh_ref, sc_ref, g_ref, wu_ref, cw_ref, wd_ref, out_ref, hb_s, up_s, *, per):
    i = pl.program_id(0)
    tm = x_ref.shape[0]
    nf = wd_ref.shape[0]
    nw, sh, sc = nw_ref[...], sh_ref[0], sc_ref[0]
    x = x_ref[...]
    live = (i % per > 0).astype(F32)
    hb_s[0:FFN_HALO, :] = (_modulate(xh_ref[...], nw, sc, sh) * live).astype(BF16)
    hb_s[FFN_HALO:, :] = _modulate(x, nw, sc, sh).astype(BF16)
    hb = hb_s[...]

    def conv(slot, half, w):
        y = w[FFN_CONV - 1:FFN_CONV, :] * up_s[slot, half, FFN_HALO:, :]
        for s in range(1, FFN_CONV):
            y = y + w[FFN_CONV - 1 - s:FFN_CONV - s, :] * up_s[slot, half, FFN_HALO - s:FFN_HALO - s + tm, :]
        return y

    nslot = up_s.shape[0]

    def up(f):
        for half in range(2):
            up_s[f % nslot, half] = jnp.dot(hb, wu_ref[half, f], preferred_element_type=F32)

    acc = jnp.zeros((tm, out_ref.shape[-1]), F32)
    for f in range(min(FFN_AHEAD, nf)):
        up(f)
    for f in range(nf):
        if f + FFN_AHEAD < nf:
            up(f + FFN_AHEAD)
        act = _silu(conv(f % nslot, 0, cw_ref[0, f])) * conv(f % nslot, 1, cw_ref[1, f])
        acc = acc + jnp.dot(act.astype(BF16), wd_ref[f], preferred_element_type=F32)
    out_ref[...] = x + g_ref[0] * acc


def conv_ffn(x2, seq, nw, shift, scale, gate, w_up, conv_w, w_down, tm=512, cw=256):
    n, d = x2.shape
    dff = w_down.shape[0]
    nf = dff // cw
    per = seq // tm
    wu = w_up.astype(BF16).reshape(d, 2, nf, cw).transpose(1, 2, 0, 3)
    cwt = conv_w.reshape(FFN_CONV, 2, nf, cw).transpose(1, 2, 0, 3)
    wd = w_down.astype(BF16).reshape(nf, cw, d)
    row = lambda i: (i, 0)
    bat = lambda i: (i // per, 0, 0)
    hpt = tm // FFN_HALO
    return pl.pallas_call(
        functools.partial(_ffn_kernel, per=per),
        grid=(n // tm,),
        in_specs=[pl.BlockSpec((tm, d), row),
                  pl.BlockSpec((FFN_HALO, d), lambda i: (jnp.maximum(i * hpt - 1, 0), 0)),
                  _resident((1, d)), pl.BlockSpec((1, 1, d), bat), pl.BlockSpec((1, 1, d), bat),
                  pl.BlockSpec((1, 1, d), bat),
                  _resident(wu.shape), _resident(cwt.shape), _resident(wd.shape)],
        out_specs=pl.BlockSpec((tm, d), row),
        out_shape=jax.ShapeDtypeStruct((n, d), F32),
        scratch_shapes=[pltpu.VMEM((FFN_HALO + tm, d), BF16),
                        pltpu.VMEM((FFN_AHEAD + 1, 2, FFN_HALO + tm, cw), F32)],
        compiler_params=_cparams(("parallel",)),
        name="ffn",
    )(x2, x2, nw.reshape(1, d), shift, scale, gate, wu, cwt, wd)


def _alibi_slopes(n):
    return jnp.asarray(2.0 ** (-8.0 * np.arange(1, n + 1) / n), dtype=F32)


def _pad_cols(w, mult):
    pad = (-w.shape[1]) % mult
    return jnp.pad(w, ((0, 0), (0, pad)))


def _rec_weight_layout(w):
    dq = DN_HEADS * DN_DK
    nu = S5_WIDTH
    w2 = _pad_cols(jnp.concatenate([w[:, nu:nu + 4 * dq], w[:, :nu], w[:, nu + 4 * dq:]], axis=1), LANE)
    cols = (0, dq, 2 * dq, 3 * dq, 4 * dq + nu)
    return w2, cols, 4 * dq


def kernel(x, c, ada_w, ada_b, norm_mix, norm_ffn, attn_w_in, attn_q_norm_a, attn_k_norm_a, attn_q_norm_b,
           attn_k_norm_b, attn_sinks, attn_w_out, rec_w_in, s5_lambda_re, s5_lambda_im, s5_log_dt, s5_b_re,
           s5_b_im, s5_c_re, s5_c_im, s5_d, s5_glu_w, s5_glu_b, dn_conv, dn_a_log, dn_dt_bias, dn_out_norm,
           rec_w_out, ffn_w_up, ffn_conv, ffn_w_down):
    bsz, seq, d = x.shape
    depth = ada_w.shape[0]
    mod = adaln(c, ada_w, ada_b)
    x2 = x.reshape(bsz * seq, d)
    slopes = _alibi_slopes(N_ATTN_HEADS)
    for layer in range(depth):
        sh1, sc1, g1, sh2, sc2, g2 = [mod[layer, :, j * d:(j + 1) * d].reshape(bsz, 1, d) for j in range(6)]
        i = layer // 2
        if layer % 2 == 0:
            qkv = inproj(x2, seq, norm_mix[layer], sh1, sc1, attn_w_in[i].astype(BF16))
            qkv = qkv.reshape(bsz, seq, -1)
            oa, ob = attention(qkv, slopes, attn_sinks[i], attn_q_norm_a[i], attn_k_norm_a[i],
                               attn_q_norm_b[i], attn_k_norm_b[i])
            x2 = mix_out(x2, seq, g1, oa.reshape(bsz * seq, -1), ob.reshape(bsz * seq, -1),
                         attn_w_out[i].astype(BF16), "attn_out")
        else:
            w_rec, rec_cols, ucol = _rec_weight_layout(rec_w_in[i])
            rec = inproj(x2, seq, norm_mix[layer], sh1, sc1, w_rec.astype(BF16))
            rec = rec.reshape(bsz, seq, -1)
            yc = s5_mixer(rec, s5_lambda_re[i], s5_lambda_im[i], s5_log_dt[i], s5_b_re[i], s5_b_im[i],
                          s5_c_re[i], s5_c_im[i], s5_d[i], s5_glu_w[i], s5_glu_b[i], ucol=ucol)
            yd = deltanet_mixer(rec, dn_conv[i], dn_a_log[i], dn_dt_bias[i], dn_out_norm[i], rec_cols)
            x2 = mix_out(x2, seq, g1, yc.reshape(bsz * seq, -1), yd.reshape(bsz * seq, -1),
                         rec_w_out[i].astype(BF16), "rec_out")
        x2 = conv_ffn(x2, seq, norm_ffn[layer], sh2, sc2, g2, ffn_w_up[layer], ffn_conv[layer], ffn_w_down[layer])
    return x2.reshape(bsz, seq, d)
```

```python
import functools
import math

import numpy as np
import jax
import jax.numpy as jnp
from jax import lax
from jax.experimental import pallas as pl
from jax.experimental.pallas import tpu as pltpu

F32 = jnp.float32
BF16 = jnp.bfloat16
HIGHEST = lax.Precision.HIGHEST
EPS = 1e-6
NEG = -1e30

HEAD_DIM = 64
BLOCK = 128
A_Q_HEADS = 8
A_KV_HEADS = 2
A_WINDOW = 128
B_HEADS = 8
B_BRANCHES = ((128, 1), (512, 4), (2048, 16))
N_ATTN_HEADS = A_Q_HEADS + B_HEADS
S5_GROUP = 16
S5_GROUPS = 16
S5_WIDTH = 256
S5_STATE = 64
DN_HEADS = 6
DN_DK = 128
DN_CONV = 4
DN_CHUNK = 64
FFN_CONV = 3

V7X_VMEM_LIMIT = 60 * 1024 * 1024
LANE = 128


def _cparams(sem, vmem=V7X_VMEM_LIMIT):
    return pltpu.CompilerParams(dimension_semantics=sem, vmem_limit_bytes=vmem)


def _resident(shape):
    n = len(shape)
    return pl.BlockSpec(shape, lambda *_: (0,) * n, pipeline_mode=pl.Buffered(1))


def _sigmoid(x):
    return 1.0 / (1.0 + jnp.exp(-x))


def _silu(x):
    return x * _sigmoid(x)


def _modulate(x, nw, scale, shift):
    ms = jnp.mean(x * x, axis=-1, keepdims=True)
    y = x * lax.rsqrt(ms + EPS) * nw
    return y * (1.0 + scale) + shift


def _adaln_kernel(c_ref, w_ref, b_ref, o_ref):
    c = c_ref[...]
    o_ref[0] = jnp.dot(_silu(c), w_ref[0], precision=HIGHEST, preferred_element_type=F32) + b_ref[0]


def adaln(c, ada_w, ada_b, tn=1536):
    depth, d, n = ada_w.shape
    bsz = c.shape[0]
    cp = jnp.zeros((8, d), F32).at[:bsz].set(c)
    out = pl.pallas_call(
        _adaln_kernel,
        grid=(depth, n // tn),
        in_specs=[pl.BlockSpec((8, d), lambda l, j: (0, 0)),
                  pl.BlockSpec((1, d, tn), lambda l, j: (l, 0, j)),
                  pl.BlockSpec((1, 1, tn), lambda l, j: (l, 0, j))],
        out_specs=pl.BlockSpec((1, 8, tn), lambda l, j: (l, 0, j)),
        out_shape=jax.ShapeDtypeStruct((depth, 8, n), F32),
        compiler_params=_cparams(("arbitrary", "arbitrary")),
        name="adaln",
    )(cp, ada_w, ada_b.reshape(depth, 1, n))
    return out[:, :bsz]


def _inproj_kernel(x_ref, nw_ref, sh_ref, sc_ref, w_ref, o_ref, *, ncol):
    h = _modulate(x_ref[...], nw_ref[...], sc_ref[0], sh_ref[0]).astype(BF16)
    nout = o_ref.shape[-1]
    for j in range(nout // ncol):
        o_ref[:, j * ncol:(j + 1) * ncol] = jnp.dot(h, w_ref[:, j * ncol:(j + 1) * ncol],
                                                    preferred_element_type=F32)


def inproj(x2, seq, nw, shift, scale, w, tm=512, ncol=1152):
    n, d = x2.shape
    nout = w.shape[1]
    per = seq // tm
    return pl.pallas_call(
        functools.partial(_inproj_kernel, ncol=ncol),
        grid=(n // tm,),
        in_specs=[pl.BlockSpec((tm, d), lambda i: (i, 0)),
                  _resident((1, d)),
                  pl.BlockSpec((1, 1, d), lambda i: (i // per, 0, 0)),
                  pl.BlockSpec((1, 1, d), lambda i: (i // per, 0, 0)),
                  _resident((d, nout))],
        out_specs=pl.BlockSpec((tm, nout), lambda i: (i, 0)),
        out_shape=jax.ShapeDtypeStruct((n, nout), F32),
        compiler_params=_cparams(("parallel",)),
        name="inproj",
    )(x2, nw.reshape(1, d), shift, scale, w)


ATT_SPAN = 2048
ATT_GROUP = 8
ATT_PREP = 512


def _static_or_multiple(x, k):
    return x if isinstance(x, int) else pl.multiple_of(x, k)


def _attn_kernel(slopes_ref, sinks_ref, q_ref, k_ref, v_ref, qw_ref, kw_ref, o_ref, *scratch,
                 branches, head0, with_sinks, dup_kv):
    p = pl.program_id(1)
    t = pl.program_id(2)
    span = q_ref.shape[1]
    nbr = len(branches)
    assert branches[0][0] == 1
    need32 = nbr > 1
    it = iter(scratch)
    qn_s, kn_s = (next(it), next(it)) if need32 else (None, None)
    per = [(next(it), next(it), next(it), next(it)) for _ in branches]
    acc_s, m_s, l_s = (next(it), next(it), next(it)) if nbr > 1 else (None, None, None)

    lane = lax.broadcasted_iota(jnp.int32, (1, LANE), 1)
    low = lane < HEAD_DIM
    li = lax.broadcasted_iota(jnp.int32, (LANE, LANE), 0) // HEAD_DIM
    lj = lax.broadcasted_iota(jnp.int32, (LANE, LANE), 1) // HEAD_DIM
    same_head = (li == lj).astype(BF16)

    for (d, _), (_, ks, vs, _) in zip(branches, per):
        n_d = span // d

        @pl.when(t == 0)
        def _():
            for r in range(d):
                ks[r, 0:BLOCK, :] = jnp.zeros((BLOCK, LANE), BF16)
                for j in range(2):
                    vs[j, r, 0:BLOCK, :] = jnp.zeros((BLOCK, LANE), BF16)

        @pl.when(t > 0)
        def _():
            for r in range(d):
                ks[r, 0:BLOCK, :] = ks[r, n_d:n_d + BLOCK, :]
                for j in range(2):
                    vs[j, r, 0:BLOCK, :] = vs[j, r, n_d:n_d + BLOCK, :]

    qi = lax.broadcasted_iota(jnp.int32, (BLOCK, 2 * BLOCK), 0)
    si = lax.broadcasted_iota(jnp.int32, (BLOCK, 2 * BLOCK), 1)
    dist = BLOCK + qi - si
    for (d, max_dist), (_, _, _, bias) in zip(branches, per):
        valid = (dist >= 0) & (dist <= max_dist)
        valid0 = valid & (si >= BLOCK)
        distf = (d * dist).astype(F32)
        for j in range(2):
            b = -slopes_ref[head0 + 2 * p + j] * distf
            bias[0, j] = jnp.where(valid0, b, NEG)
            bias[1, j] = jnp.where(valid, b, NEG)

    def head_norm(x, w):
        x2 = x * x
        hi = x2.astype(BF16)
        lo = (x2 - hi.astype(F32)).astype(BF16)
        ss = jnp.dot(hi, same_head, preferred_element_type=F32) + jnp.dot(lo, same_head, preferred_element_type=F32)
        return x * lax.rsqrt(ss * (1.0 / HEAD_DIM) + EPS) * w

    def both_halves(x):
        swapped = pltpu.roll(x, HEAD_DIM, 1)
        return jnp.where(lane // HEAD_DIM == p // 2, x, swapped)

    def put_q(qs, rows, x):
        qs[0, rows, :] = jnp.where(low, x, 0.0).astype(BF16)
        qs[1, rows, :] = jnp.where(low, 0.0, x).astype(BF16)

    def put_v(vs, r, rows, x):
        vs[0, r, rows, :] = jnp.where(low, x, 1.0).astype(BF16)
        vs[1, r, rows, :] = jnp.where(low, 1.0, x).astype(BF16)

    qw, kw = qw_ref[...], kw_ref[...]
    qs1, ks1, vs1, _ = per[0]

    def prep(c, _):
        r0 = pl.multiple_of(c * ATT_PREP, ATT_PREP)
        rows = pl.ds(r0, ATT_PREP)
        krows = pl.ds(pl.multiple_of(BLOCK + r0, BLOCK), ATT_PREP)
        qn = head_norm(q_ref[0, rows, :], qw) * (HEAD_DIM ** -0.5)
        xk, xv = k_ref[0, rows, :], v_ref[0, rows, :]
        if dup_kv:
            xk, xv = both_halves(xk), both_halves(xv)
        kn = head_norm(xk, kw)
        if need32:
            qn_s[rows, :] = qn
            kn_s[rows, :] = kn
        put_q(qs1, rows, qn)
        ks1[0, krows, :] = kn.astype(BF16)
        put_v(vs1, 0, krows, xv)
        return 0

    lax.fori_loop(0, span // ATT_PREP, prep, 0)
    for (d, _), (qs, ks, vs, _) in zip(branches[1:], per[1:]):
        n_d = span // d
        for r in range(d):
            sub = pl.ds(r, n_d, stride=d)
            put_q(qs, slice(r * n_d, (r + 1) * n_d), qn_s[sub, :])
            ks[r, BLOCK:BLOCK + n_d, :] = kn_s[sub, :].astype(BF16)
            put_v(vs, r, slice(BLOCK, BLOCK + n_d), v_ref[0, sub, :])

    def process(bi, combos):
        d, _ = branches[bi]
        qs, ks, vs, bias = per[bi]
        n_d = span // d
        qv, kv, vv, bs, out_rows = [], [], [], [], []
        for r, n in combos:
            qrow = _static_or_multiple(r * n_d + n * BLOCK, BLOCK)
            krow = _static_or_multiple(n * BLOCK, BLOCK)
            bidx = jnp.where(jnp.logical_and(t == 0, n == 0), 0, 1)
            for j in range(2):
                qv.append(qs[j, pl.ds(qrow, BLOCK), :])
                kv.append(ks[r, pl.ds(krow, 2 * BLOCK), :])
                vv.append(vs[j, r, pl.ds(krow, 2 * BLOCK), :])
                bs.append(bias[bidx, j])
            out_rows.append(pl.ds(krow, BLOCK) if d == 1 else pl.ds(r + d * BLOCK * n, BLOCK, stride=d))
        s = [lax.dot_general(q, k, (((1,), (1,)), ((), ())), preferred_element_type=F32) + b
             for q, k, b in zip(qv, kv, bs)]
        m = [jnp.max(x, axis=-1, keepdims=True) for x in s]
        pr = [jnp.exp(x - mm).astype(BF16) for x, mm in zip(s, m)]
        acc = [jnp.dot(x, v, preferred_element_type=F32) for x, v in zip(pr, vv)]
        for ci, rows in enumerate(out_rows):
            a0, a1, m0, m1 = acc[2 * ci], acc[2 * ci + 1], m[2 * ci], m[2 * ci + 1]
            num = jnp.where(low, a0, a1)
            den = pltpu.roll(jnp.where(low, a1, a0), HEAD_DIM, 1)
            mx = jnp.where(low, m0, m1)
            if nbr == 1:
                if with_sinks:
                    sk = jnp.where(low, sinks_ref[2 * p], sinks_ref[2 * p + 1])
                    den = den + jnp.exp(sk - mx)
                o_ref[0, rows, :] = num / den
            elif bi == 0:
                acc_s[rows, :] = num
                l_s[rows, :] = den
                m_s[rows, :] = mx
            else:
                m_old = m_s[rows, :]
                m_new = jnp.maximum(m_old, mx)
                w_old, w_new = jnp.exp(m_old - m_new), jnp.exp(mx - m_new)
                num = w_old * acc_s[rows, :] + w_new * num
                den = w_old * l_s[rows, :] + w_new * den
                if bi == nbr - 1:
                    o_ref[0, rows, :] = num / den
                else:
                    acc_s[rows, :] = num
                    l_s[rows, :] = den
                    m_s[rows, :] = m_new

    for bi, (d, _) in enumerate(branches):
        nblk = span // d // BLOCK

        def group(g, _, bi=bi, nblk=nblk):
            if nblk >= ATT_GROUP:
                base = g * ATT_GROUP
                r = base // nblk
                combos = [(r, base % nblk + i) for i in range(ATT_GROUP)]
            else:
                per_grp = ATT_GROUP // nblk
                combos = [(g * per_grp + i // nblk, i % nblk) for i in range(ATT_GROUP)]
            process(bi, combos)
            return 0

        lax.fori_loop(0, span // BLOCK // ATT_GROUP, group, 0)


def _attention_call(qkv, slopes, sinks, qw, kw, *, branches, qblk, kblk, vblk, shared_kv, head0, with_sinks, name):
    bsz, seq, _ = qkv.shape
    span = min(seq, ATT_SPAN)
    npair = 4
    kmap = (lambda b, p, t: (b, t, kblk)) if shared_kv else (lambda b, p, t: (b, t, kblk + p))
    vmap = (lambda b, p, t: (b, t, vblk)) if shared_kv else (lambda b, p, t: (b, t, vblk + p))
    smem = pl.BlockSpec(memory_space=pltpu.SMEM)
    scratch = []
    if len(branches) > 1:
        scratch += [pltpu.VMEM((span, LANE), F32), pltpu.VMEM((span, LANE), F32)]
    for d, _ in branches:
        n_d = span // d
        scratch += [pltpu.VMEM((2, span, LANE), BF16), pltpu.VMEM((d, BLOCK + n_d, LANE), BF16),
                    pltpu.VMEM((2, d, BLOCK + n_d, LANE), BF16), pltpu.VMEM((2, 2, BLOCK, 2 * BLOCK), F32)]
    if len(branches) > 1:
        scratch += [pltpu.VMEM((span, LANE), F32)] * 3
    tile2 = lambda w: jnp.tile(w.reshape(1, HEAD_DIM), (1, 2))
    return pl.pallas_call(
        functools.partial(_attn_kernel, branches=branches, head0=head0, with_sinks=with_sinks, dup_kv=shared_kv),
        grid=(bsz, npair, seq // span),
        in_specs=[smem, smem,
                  pl.BlockSpec((1, span, LANE), lambda b, p, t: (b, t, qblk + p)),
                  pl.BlockSpec((1, span, LANE), kmap),
                  pl.BlockSpec((1, span, LANE), vmap),
                  _resident((1, LANE)), _resident((1, LANE))],
        out_specs=pl.BlockSpec((1, span, LANE), lambda b, p, t: (b, t, p)),
        out_shape=jax.ShapeDtypeStruct((bsz, seq, npair * LANE), F32),
        scratch_shapes=scratch,
        compiler_params=_cparams(("parallel", "parallel", "arbitrary")),
        name=name,
    )(slopes, sinks, qkv, qkv, qkv, tile2(qw), tile2(kw))


def attention(qkv, slopes, sinks, qw_a, kw_a, qw_b, kw_b):
    oa = _attention_call(qkv, slopes, sinks, qw_a, kw_a, branches=((1, A_WINDOW - 1),), qblk=0, kblk=4, vblk=5,
                         shared_kv=True, head0=0, with_sinks=True, name="attn_a")
    ob = _attention_call(qkv, slopes, sinks, qw_b, kw_b, branches=tuple((d, w // d) for w, d in B_BRANCHES),
                         qblk=6, kblk=10, vblk=14, shared_kv=False, head0=A_Q_HEADS, with_sinks=False, name="attn_b")
    return oa, ob


def _mix_out_kernel(x_ref, g_ref, ya_ref, yb_ref, w_ref, out_ref):
    na = ya_ref.shape[-1]
    y = jnp.dot(ya_ref[...].astype(BF16), w_ref[:na, :], preferred_element_type=F32)
    y = y + jnp.dot(yb_ref[...].astype(BF16), w_ref[na:, :], preferred_element_type=F32)
    out_ref[...] = x_ref[...] + g_ref[0] * y


def mix_out(x2, seq, gate, ya, yb, w, name, tm=512):
    n, d = x2.shape
    per = seq // tm
    row = lambda i: (i, 0)
    return pl.pallas_call(
        _mix_out_kernel,
        grid=(n // tm,),
        in_specs=[pl.BlockSpec((tm, d), row), pl.BlockSpec((1, 1, d), lambda i: (i // per, 0, 0)),
                  pl.BlockSpec((tm, ya.shape[-1]), row), pl.BlockSpec((tm, yb.shape[-1]), row), _resident(w.shape)],
        out_specs=pl.BlockSpec((tm, d), row),
        out_shape=jax.ShapeDtypeStruct((n, d), F32),
        compiler_params=_cparams(("parallel",)),
        name=name,
    )(x2, gate, ya, yb, w)


def _gelu_tanh(x):
    return 0.5 * x * (1.0 + jnp.tanh(math.sqrt(2.0 / math.pi) * (x + 0.044715 * (x * x * x))))


def _s5_kernel(u_ref, e_ref, pr_ref, pi_ref, cr_ref, ci_ref, d_ref, gw_ref, gb_ref, o_ref, xr_s, xi_s, car_s):
    t = pl.program_id(1)
    tc = u_ref.shape[1]
    ns = pr_ref.shape[1]

    @pl.when(t == 0)
    def _():
        car_s[...] = jnp.zeros_like(car_s)

    u = u_ref[0]
    e = jnp.dot(u.astype(BF16), e_ref[...], preferred_element_type=F32)
    xr, xi = e[:, :ns], e[:, ns:]
    row = lax.broadcasted_iota(jnp.int32, (tc, ns), 0) % 8
    for s in (1, 2, 4):
        ar, ai = pr_ref[s - 1:s, :], pi_ref[s - 1:s, :]
        keep = row >= s
        sr = jnp.where(keep, pltpu.roll(xr, s, 0), 0.0)
        si = jnp.where(keep, pltpu.roll(xi, s, 0), 0.0)
        xr, xi = xr + (ar * sr - ai * si), xi + (ar * si + ai * sr)
    xr_s[...] = xr
    xi_s[...] = xi
    pr, pi = pr_ref[...], pi_ref[...]

    def grp(k, carry):
        cr, ci = carry
        r0 = pl.multiple_of(k * 8, 8)
        nr = xr_s[pl.ds(r0, 8), :] + (pr * cr - pi * ci)
        ni = xi_s[pl.ds(r0, 8), :] + (pr * ci + pi * cr)
        xr_s[pl.ds(r0, 8), :] = nr
        xi_s[pl.ds(r0, 8), :] = ni
        return nr[7:8, :], ni[7:8, :]

    cr, ci = lax.fori_loop(0, tc // 8, grp, (car_s[0:1, :], car_s[1:2, :]))
    car_s[0:1, :] = cr
    car_s[1:2, :] = ci
    y = (jnp.dot(xr_s[...].astype(BF16), cr_ref[...], preferred_element_type=F32)
         - jnp.dot(xi_s[...].astype(BF16), ci_ref[...], preferred_element_type=F32)
         + d_ref[...] * u)
    g = _gelu_tanh(y)
    z = jnp.dot(g.astype(BF16), gw_ref[...], preferred_element_type=F32) + gb_ref[...]
    o_ref[0] = g * _sigmoid(z)


def s5_mixer(rec, lam_re, lam_im, log_dt, b_re, b_im, c_re, c_im, d_skip, glu_w, glu_b, ucol=0, tc=512):
    bsz, seq, _ = rec.shape
    ublk = ucol // S5_WIDTH
    ns = S5_GROUPS * S5_STATE
    dt = jnp.exp(log_dt)[:, None]
    lr, li = lam_re, lam_im
    mag, ang = jnp.exp(lr * dt), li * dt
    ab_re, ab_im = mag * jnp.cos(ang), mag * jnp.sin(ang)
    nr, ni = ab_re - 1.0, ab_im
    den = lr * lr + li * li
    f_re = (nr * lr + ni * li) / den
    f_im = (ni * lr - nr * li) / den
    kk = jnp.arange(1, 9, dtype=F32)[:, None, None]
    pmag = jnp.exp(kk * (lr * dt)[None])
    p_re = (pmag * jnp.cos(kk * ang[None])).reshape(8, ns)
    p_im = (pmag * jnp.sin(kk * ang[None])).reshape(8, ns)
    e_re = f_re[..., None] * b_re - f_im[..., None] * b_im
    e_im = f_re[..., None] * b_im + f_im[..., None] * b_re
    eye = jnp.eye(S5_GROUPS, dtype=F32)

    def bd_in(m):
        return jnp.einsum("gpi,gh->gihp", m, eye).reshape(S5_WIDTH, ns)

    def bd_out(m):
        return jnp.einsum("gip,gh->gphi", m, eye).reshape(ns, S5_WIDTH)

    e_mat = jnp.concatenate([bd_in(e_re), bd_in(e_im)], axis=1).astype(BF16)
    return pl.pallas_call(
        _s5_kernel,
        grid=(bsz, seq // tc),
        in_specs=[pl.BlockSpec((1, tc, S5_WIDTH), lambda b, t: (b, t, ublk)),
                  _resident((S5_WIDTH, 2 * ns)), _resident((8, ns)), _resident((8, ns)),
                  _resident((ns, S5_WIDTH)), _resident((ns, S5_WIDTH)),
                  _resident((1, S5_WIDTH)), _resident((S5_WIDTH, S5_WIDTH)), _resident((1, S5_WIDTH))],
        out_specs=pl.BlockSpec((1, tc, S5_WIDTH), lambda b, t: (b, t, 0)),
        out_shape=jax.ShapeDtypeStruct((bsz, seq, S5_WIDTH), F32),
        scratch_shapes=[pltpu.VMEM((tc, ns), F32), pltpu.VMEM((tc, ns), F32), pltpu.VMEM((8, ns), F32)],
        compiler_params=_cparams(("parallel", "arbitrary")),
        name="s5",
    )(rec, e_mat, p_re, p_im, bd_out(c_re).astype(BF16), bd_out(c_im).astype(BF16),
      d_skip.reshape(1, S5_WIDTH), glu_w.astype(BF16), glu_b.reshape(1, S5_WIDTH))


def _dot_bf(a, b):
    return jnp.dot(a.astype(BF16), b.astype(BF16), preferred_element_type=F32)


def _dot_nt_bf(a, b):
    return lax.dot_general(a.astype(BF16), b.astype(BF16), (((1,), (1,)), ((), ())), preferred_element_type=F32)


def _dot_tn_bf(a, b):
    return lax.dot_general(a.astype(BF16), b.astype(BF16), (((0,), (0,)), ((), ())), preferred_element_type=F32)


def _split_bf(x):
    hi = x.astype(BF16)
    return hi, (x - hi.astype(F32)).astype(BF16)


def _dot_x3(a, b):
    ah, al = _split_bf(a)
    bh, bl = _split_bf(b)
    return (jnp.dot(ah, bh, preferred_element_type=F32) + jnp.dot(ah, bl, preferred_element_type=F32)
            + jnp.dot(al, bh, preferred_element_type=F32))


def _dot_01(m01, x):
    hi = x.astype(BF16)
    r = x - hi.astype(F32)
    mid = r.astype(BF16)
    lo = (r - mid.astype(F32)).astype(BF16)
    return (jnp.dot(m01, hi, preferred_element_type=F32) + jnp.dot(m01, mid, preferred_element_type=F32)
            + jnp.dot(m01, lo, preferred_element_type=F32))


def _dn_kernel(q_ref, k_ref, v_ref, qh_ref, kh_ref, vh_ref, z_ref, ab_ref, wq_ref, wk_ref, wv_ref, nw_ref,
               alog_ref, dtb_ref, o_ref, s_s):
    t = pl.program_id(1)
    tc = q_ref.shape[1]
    c = DN_CHUNK

    @pl.when(t == 0)
    def _():
        s_s[...] = jnp.zeros_like(s_s)

    live = (t > 0).astype(F32)

    def conv_silu(x_ref, halo_ref, w_ref):
        xe = jnp.concatenate([halo_ref[0] * live, x_ref[0]], axis=0)
        w = w_ref[...]
        y = w[DN_CONV - 1:DN_CONV, :] * xe[8:, :]
        for s in range(1, DN_CONV):
            y = y + w[DN_CONV - 1 - s:DN_CONV - s, :] * pltpu.roll(xe, s, 0)[8:, :]
        return _silu(y)

    def head_l2(x2, scale):
        out = []
        for j in range(DN_HEADS):
            x = x2[:, j * DN_DK:(j + 1) * DN_DK]
            out.append(x * lax.rsqrt(jnp.sum(x * x, axis=-1, keepdims=True) + EPS) * scale)
        return out

    qn = head_l2(conv_silu(q_ref, qh_ref, wq_ref), DN_DK ** -0.5)
    kn = head_l2(conv_silu(k_ref, kh_ref, wk_ref), 1.0)
    v2 = conv_silu(v_ref, vh_ref, wv_ref)
    ab = ab_ref[0]
    bfull = _sigmoid(ab)
    xa = ab + dtb_ref[...]
    gfull = -jnp.exp(alog_ref[...]) * (jnp.maximum(xa, 0.0) + jnp.log(1.0 + jnp.exp(-jnp.abs(xa))))

    ri = lax.broadcasted_iota(jnp.int32, (c, c), 0)
    ci = lax.broadcasted_iota(jnp.int32, (c, c), 1)
    causal = ri >= ci
    strict = ri > ci
    tril = causal.astype(BF16)
    triu = (ri <= ci).astype(F32)
    ones = jnp.ones((c, c), BF16)

    nck = tc // c
    probs = [(i, j) for i in range(nck) for j in range(DN_HEADS)]
    rows = lambda x, i: x[i * c:(i + 1) * c]
    qc = [rows(qn[j], i) for i, j in probs]
    kc = [rows(kn[j], i) for i, j in probs]
    vc = [rows(v2[:, j * DN_DK:(j + 1) * DN_DK], i) for i, j in probs]
    bc = [rows(bfull[:, DN_HEADS + j:DN_HEADS + j + 1], i) for i, j in probs]
    gb = [jnp.broadcast_to(rows(gfull[:, j:j + 1], i), (c, c)) for i, j in probs]
    cs_l = jnp.concatenate([tril, -ones], axis=1)
    zero = jnp.zeros((c, c), F32)
    cs = [_dot_01(cs_l, jnp.concatenate([jnp.concatenate([x, x], axis=1),
                                         jnp.concatenate([x * triu, zero], axis=1)], axis=0)) for x in gb]
    gdiff = [x[:, :c] for x in cs]
    gi = [x[:, c:c + 1] for x in cs]
    glast = [x[c - 1:c, c:c + 1] for x in cs]
    qkk = [_dot_nt_bf(jnp.concatenate([x, y], axis=0), y) for x, y in zip(qc, kc)]
    gamma = [jnp.where(causal, jnp.exp(jnp.where(causal, x, 0.0)), 0.0) for x in gdiff]
    nm = [jnp.where(strict, b * x[c:] * gm, 0.0) for b, x, gm in zip(bc, qkk, gamma)]
    qk = [x[:c] for x in qkk]
    eye = (ri == ci).astype(F32)
    inv = [eye - x for x in nm]
    pw = [_dot_x3(x, x) for x in nm]
    span = 2
    while span < c:
        inv = [x + _dot_x3(x, y) for x, y in zip(inv, pw)]
        span *= 2
        if span < c:
            pw = [_dot_x3(y, y) for y in pw]
    eg = [jnp.exp(x) for x in gi]
    qk = [x * gm for x, gm in zip(qk, gamma)]
    kd = [k * jnp.exp(gl - g) for k, gl, g in zip(kc, glast, gi)]
    w2 = [jnp.concatenate([_dot_tn_bf(x, t), _dot_bf(y, t)], axis=0).astype(BF16) for x, y, t in zip(kd, qk, inv)]
    ke = [k * e for k, e in zip(kc, eg)]
    ke_hi = [x.astype(BF16) for x in ke]
    ke_lo = [(x - h.astype(F32)).astype(BF16) for x, h in zip(ke, ke_hi)]
    l1 = [jnp.concatenate([h, lo, (q * e).astype(BF16)], axis=0) for h, lo, q, e in zip(ke_hi, ke_lo, qc, eg)]
    decay = [jnp.exp(x) for x in glast]

    nw = nw_ref[...]
    states = [s_s[j] for j in range(DN_HEADS)]
    for i in range(nck):
        sl = slice(i * c, (i + 1) * c)
        ns = [i * DN_HEADS + j for j in range(DN_HEADS)]
        s_hi = [s.astype(BF16) for s in states]
        s_lo = [(s - h.astype(F32)).astype(BF16) for s, h in zip(states, s_hi)]
        p1 = [jnp.dot(l1[n], h, preferred_element_type=F32) for n, h in zip(ns, s_hi)]
        p2 = [jnp.dot(ke_hi[n], lo, preferred_element_type=F32) for n, lo in zip(ns, s_lo)]
        res = [bc[n] * (vc[n] - (a[:c] + a[c:2 * c] + b)) for n, a, b in zip(ns, p1, p2)]
        p3 = [jnp.dot(w2[n], r.astype(BF16), preferred_element_type=F32) for n, r in zip(ns, res)]
        outs = []
        for j in range(DN_HEADS):
            o = p1[j][2 * c:] + p3[j][DN_DK:]
            states[j] = states[j] * decay[ns[j]] + p3[j][:DN_DK]
            outs.append(o * lax.rsqrt(jnp.mean(o * o, axis=-1, keepdims=True) + EPS) * nw)
        o_ref[0, sl, :] = jnp.concatenate(outs, axis=-1) * _silu(z_ref[0, sl, :])
    for j in range(DN_HEADS):
        s_s[j] = states[j]


def deltanet_mixer(rec, conv_w, a_log, dt_bias, out_norm, cols, tc=256):
    bsz, seq, _ = rec.shape
    cq, ck, cv, cz, cab = cols
    hb = tc // 8
    dq = DN_HEADS * DN_DK

    def col(c0):
        return pl.BlockSpec((1, tc, dq), lambda b, t: (b, t, c0 // dq))

    def halo(c0):
        return pl.BlockSpec((1, 8, dq), lambda b, t: (b, jnp.maximum(t * hb - 1, 0), c0 // dq))

    def wcol(c0):
        return pl.BlockSpec((DN_CONV, dq), lambda b, t: (0, c0 // dq))

    def lanes(x):
        return jnp.zeros((1, LANE), F32).at[0, :DN_HEADS].set(x)

    return pl.pallas_call(
        _dn_kernel,
        grid=(bsz, seq // tc),
        in_specs=[col(cq), col(ck), col(cv), halo(cq), halo(ck), halo(cv), col(cz),
                  pl.BlockSpec((1, tc, LANE), lambda b, t: (b, t, cab // LANE)),
                  wcol(0), wcol(dq), wcol(2 * dq), _resident((1, DN_DK)), _resident((1, LANE)), _resident((1, LANE))],
        out_specs=pl.BlockSpec((1, tc, dq), lambda b, t: (b, t, 0)),
        out_shape=jax.ShapeDtypeStruct((bsz, seq, dq), F32),
        scratch_shapes=[pltpu.VMEM((DN_HEADS, DN_DK, DN_DK), F32)],
        compiler_params=_cparams(("parallel", "arbitrary")),
        name="deltanet",
    )(rec, rec, rec, rec, rec, rec, rec, rec, conv_w, conv_w, conv_w, out_norm.reshape(1, DN_DK),
      lanes(a_log), lanes(dt_bias))


FFN_HALO = 16
FFN_AHEAD = 3


def _ffn_kernel(x_ref, xh_ref, nw_ref, sh_ref, sc_ref, g_ref, wu_ref, cw_ref, wd_ref, out_ref, hb_s, up_s, *, per):
    i = pl.program_id(0)
    tm = x_ref.shape[0]
    nf = wd_ref.shape[0]
    nw, sh, sc = nw_ref[...], sh_ref[0], sc_ref[0]
    x = x_ref[...]
    live = (i % per > 0).astype(F32)
    hb_s[0:FFN_HALO, :] = (_modulate(xh_ref[...], nw, sc, sh) * live).astype(BF16)
    hb_s[FFN_HALO:, :] = _modulate(x, nw, sc, sh).astype(BF16)
    hb = hb_s[...]

    def conv(slot, half, w):
        y = w[FFN_CONV - 1:FFN_CONV, :] * up_s[slot, half, FFN_HALO:, :]
        for s in range(1, FFN_CONV):
            y = y + w[FFN_CONV - 1 - s:FFN_CONV - s, :] * up_s[slot, half, FFN_HALO - s:FFN_HALO - s + tm, :]
        return y

    nslot, cw = up_s.shape[0], up_s.shape[-1]
    cols = lambda half, f: slice(half * nf * cw + f * cw, half * nf * cw + (f + 1) * cw)

    def up(f):
        for half in range(2):
            up_s[f % nslot, half] = jnp.dot(hb, wu_ref[:, cols(half, f)], preferred_element_type=F32)

    acc = jnp.zeros((tm, out_ref.shape[-1]), F32)
    for f in range(min(FFN_AHEAD, nf)):
        up(f)
    for f in range(nf):
        if f + FFN_AHEAD < nf:
            up(f + FFN_AHEAD)
        act = _silu(conv(f % nslot, 0, cw_ref[:, cols(0, f)])) * conv(f % nslot, 1, cw_ref[:, cols(1, f)])
        acc = acc + jnp.dot(act.astype(BF16), wd_ref[f], preferred_element_type=F32)
    out_ref[...] = x + g_ref[0] * acc


def conv_ffn(x2, seq, nw, shift, scale, gate, w_up, conv_w, w_down, tm=512, cw=256):
    n, d = x2.shape
    dff = w_down.shape[0]
    nf = dff // cw
    per = seq // tm
    wu = w_up.astype(BF16)
    cwt = conv_w
    wd = w_down.astype(BF16).reshape(nf, cw, d)
    row = lambda i: (i, 0)
    bat = lambda i: (i // per, 0, 0)
    hpt = tm // FFN_HALO
    return pl.pallas_call(
        functools.partial(_ffn_kernel, per=per),
        grid=(n // tm,),
        in_specs=[pl.BlockSpec((tm, d), row),
                  pl.BlockSpec((FFN_HALO, d), lambda i: (jnp.maximum(i * hpt - 1, 0), 0)),
                  _resident((1, d)), pl.BlockSpec((1, 1, d), bat), pl.BlockSpec((1, 1, d), bat),
                  pl.BlockSpec((1, 1, d), bat),
                  _resident(wu.shape), _resident(cwt.shape), _resident(wd.shape)],
        out_specs=pl.BlockSpec((tm, d), row),
        out_shape=jax.ShapeDtypeStruct((n, d), F32),
        scratch_shapes=[pltpu.VMEM((FFN_HALO + tm, d), BF16),
                        pltpu.VMEM((FFN_AHEAD + 1, 2, FFN_HALO + tm, cw), F32)],
        compiler_params=_cparams(("parallel",)),
        name="ffn",
    )(x2, x2, nw.reshape(1, d), shift, scale, gate, wu, cwt, wd)


def _alibi_slopes(n):
    return jnp.asarray(2.0 ** (-8.0 * np.arange(1, n + 1) / n), dtype=F32)


def _pad_cols(w, mult):
    pad = (-w.shape[1]) % mult
    return jnp.pad(w, ((0, 0), (0, pad)))


def _rec_weight_layout(w):
    dq = DN_HEADS * DN_DK
    nu = S5_WIDTH
    w2 = _pad_cols(jnp.concatenate([w[:, nu:nu + 4 * dq], w[:, :nu], w[:, nu + 4 * dq:]], axis=1), LANE)
    cols = (0, dq, 2 * dq, 3 * dq, 4 * dq + nu)
    return w2, cols, 4 * dq


def kernel(x, c, ada_w, ada_b, norm_mix, norm_ffn, attn_w_in, attn_q_norm_a, attn_k_norm_a, attn_q_norm_b,
           attn_k_norm_b, attn_sinks, attn_w_out, rec_w_in, s5_lambda_re, s5_lambda_im, s5_log_dt, s5_b_re,
           s5_b_im, s5_c_re, s5_c_im, s5_d, s5_glu_w, s5_glu_b, dn_conv, dn_a_log, dn_dt_bias, dn_out_norm,
           rec_w_out, ffn_w_up, ffn_conv, ffn_w_down):
    bsz, seq, d = x.shape
    depth = ada_w.shape[0]
    mod = adaln(c, ada_w, ada_b)
    x2 = x.reshape(bsz * seq, d)
    slopes = _alibi_slopes(N_ATTN_HEADS)
    for layer in range(depth):
        sh1, sc1, g1, sh2, sc2, g2 = [mod[layer, :, j * d:(j + 1) * d].reshape(bsz, 1, d) for j in range(6)]
        i = layer // 2
        if layer % 2 == 0:
            qkv = inproj(x2, seq, norm_mix[layer], sh1, sc1, attn_w_in[i].astype(BF16))
            qkv = qkv.reshape(bsz, seq, -1)
            oa, ob = attention(qkv, slopes, attn_sinks[i], attn_q_norm_a[i], attn_k_norm_a[i],
                               attn_q_norm_b[i], attn_k_norm_b[i])
            x2 = mix_out(x2, seq, g1, oa.reshape(bsz * seq, -1), ob.reshape(bsz * seq, -1),
                         attn_w_out[i].astype(BF16), "attn_out")
        else:
            w_rec, rec_cols, ucol = _rec_weight_layout(rec_w_in[i])
            rec = inproj(x2, seq, norm_mix[layer], sh1, sc1, w_rec.astype(BF16))
            rec = rec.reshape(bsz, seq, -1)
            yc = s5_mixer(rec, s5_lambda_re[i], s5_lambda_im[i], s5_log_dt[i], s5_b_re[i], s5_b_im[i],
                          s5_c_re[i], s5_c_im[i], s5_d[i], s5_glu_w[i], s5_glu_b[i], ucol=ucol)
            yd = deltanet_mixer(rec, dn_conv[i], dn_a_log[i], dn_dt_bias[i], dn_out_norm[i], rec_cols)
            x2 = mix_out(x2, seq, g1, yc.reshape(bsz * seq, -1), yd.reshape(bsz * seq, -1),
                         rec_w_out[i].astype(BF16), "rec_out")
        x2 = conv_ffn(x2, seq, norm_ffn[layer], sh2, sc2, g2, ffn_w_up[layer], ffn_conv[layer], ffn_w_down[layer])
    return x2.reshape(bsz, seq, d)
```

```python
import functools
import math

import numpy as np
import jax
import jax.numpy as jnp
from jax import lax
from jax.experimental import pallas as pl
from jax.experimental.pallas import tpu as pltpu

F32 = jnp.float32
BF16 = jnp.bfloat16
HIGHEST = lax.Precision.HIGHEST
EPS = 1e-6
NEG = -1e30

HEAD_DIM = 64
BLOCK = 128
A_Q_HEADS = 8
A_KV_HEADS = 2
A_WINDOW = 128
B_HEADS = 8
B_BRANCHES = ((128, 1), (512, 4), (2048, 16))
N_ATTN_HEADS = A_Q_HEADS + B_HEADS
S5_GROUP = 16
S5_GROUPS = 16
S5_WIDTH = 256
S5_STATE = 64
DN_HEADS = 6
DN_DK = 128
DN_CONV = 4
DN_CHUNK = 64
FFN_CONV = 3

V7X_VMEM_LIMIT = 60 * 1024 * 1024
LANE = 128


def _cparams(sem, vmem=V7X_VMEM_LIMIT):
    return pltpu.CompilerParams(dimension_semantics=sem, vmem_limit_bytes=vmem)


def _resident(shape):
    n = len(shape)
    return pl.BlockSpec(shape, lambda *_: (0,) * n, pipeline_mode=pl.Buffered(1))


def _sigmoid(x):
    return 1.0 / (1.0 + jnp.exp(-x))


def _silu(x):
    h = 0.5 * x
    return h + h * jnp.tanh(h)


def _modulate(x, nw, scale, shift):
    ms = jnp.mean(x * x, axis=-1, keepdims=True)
    y = x * lax.rsqrt(ms + EPS) * nw
    return y * (1.0 + scale) + shift


def _adaln_kernel(c_ref, w_ref, b_ref, o_ref):
    c = c_ref[...]
    o_ref[0] = jnp.dot(_silu(c), w_ref[0], precision=HIGHEST, preferred_element_type=F32) + b_ref[0]


def adaln(c, ada_w, ada_b, tn=1536):
    depth, d, n = ada_w.shape
    bsz = c.shape[0]
    cp = jnp.zeros((8, d), F32).at[:bsz].set(c)
    out = pl.pallas_call(
        _adaln_kernel,
        grid=(depth, n // tn),
        in_specs=[pl.BlockSpec((8, d), lambda l, j: (0, 0)),
                  pl.BlockSpec((1, d, tn), lambda l, j: (l, 0, j)),
                  pl.BlockSpec((1, 1, tn), lambda l, j: (l, 0, j))],
        out_specs=pl.BlockSpec((1, 8, tn), lambda l, j: (l, 0, j)),
        out_shape=jax.ShapeDtypeStruct((depth, 8, n), F32),
        compiler_params=_cparams(("arbitrary", "arbitrary")),
        name="adaln",
    )(cp, ada_w, ada_b.reshape(depth, 1, n))
    return out[:, :bsz]


def _inproj_kernel(x_ref, nw_ref, sh_ref, sc_ref, w_ref, o_ref):
    h = _modulate(x_ref[...], nw_ref[...], sc_ref[0], sh_ref[0]).astype(BF16)
    o_ref[...] = jnp.dot(h, w_ref[...], preferred_element_type=F32)


def inproj(x2, seq, nw, shift, scale, w, tm=512):
    n, d = x2.shape
    nout = w.shape[1]
    per = seq // tm
    return pl.pallas_call(
        _inproj_kernel,
        grid=(n // tm,),
        in_specs=[pl.BlockSpec((tm, d), lambda i: (i, 0)),
                  _resident((1, d)),
                  pl.BlockSpec((1, 1, d), lambda i: (i // per, 0, 0)),
                  pl.BlockSpec((1, 1, d), lambda i: (i // per, 0, 0)),
                  _resident((d, nout))],
        out_specs=pl.BlockSpec((tm, nout), lambda i: (i, 0)),
        out_shape=jax.ShapeDtypeStruct((n, nout), F32),
        compiler_params=_cparams(("parallel",)),
        name="inproj",
    )(x2, nw.reshape(1, d), shift, scale, w)


ATT_SPAN = 2048
ATT_GROUP = 8
ATT_PREP = 512


def _static_or_multiple(x, k):
    return x if isinstance(x, int) else pl.multiple_of(x, k)


def _attn_kernel(slopes_ref, sinks_ref, q_ref, k_ref, v_ref, qw_ref, kw_ref, o_ref, *scratch,
                 branches, head0, with_sinks, dup_kv):
    p = pl.program_id(1)
    t = pl.program_id(2)
    span = q_ref.shape[1]
    nbr = len(branches)
    assert branches[0][0] == 1
    need32 = nbr > 1
    it = iter(scratch)
    qn_s, kn_s = (next(it), next(it)) if need32 else (None, None)
    per = [(next(it), next(it), next(it), next(it)) for _ in branches]
    acc_s, m_s, l_s = (next(it), next(it), next(it)) if nbr > 1 else (None, None, None)

    lane = lax.broadcasted_iota(jnp.int32, (1, LANE), 1)
    low = lane < HEAD_DIM
    li = lax.broadcasted_iota(jnp.int32, (LANE, LANE), 0) // HEAD_DIM
    lj = lax.broadcasted_iota(jnp.int32, (LANE, LANE), 1) // HEAD_DIM
    same_head = (li == lj).astype(BF16)

    for (d, _), (_, ks, vs, _) in zip(branches, per):
        n_d = span // d

        @pl.when(t == 0)
        def _():
            for r in range(d):
                ks[r, 0:BLOCK, :] = jnp.zeros((BLOCK, LANE), BF16)
                for j in range(2):
                    vs[j, r, 0:BLOCK, :] = jnp.zeros((BLOCK, LANE), BF16)

        @pl.when(t > 0)
        def _():
            for r in range(d):
                ks[r, 0:BLOCK, :] = ks[r, n_d:n_d + BLOCK, :]
                for j in range(2):
                    vs[j, r, 0:BLOCK, :] = vs[j, r, n_d:n_d + BLOCK, :]

    qi = lax.broadcasted_iota(jnp.int32, (BLOCK, 2 * BLOCK), 0)
    si = lax.broadcasted_iota(jnp.int32, (BLOCK, 2 * BLOCK), 1)
    dist = BLOCK + qi - si
    for (d, max_dist), (_, _, _, bias) in zip(branches, per):
        valid = (dist >= 0) & (dist <= max_dist)
        valid0 = valid & (si >= BLOCK)
        distf = (d * dist).astype(F32)
        for j in range(2):
            b = -slopes_ref[head0 + 2 * p + j] * distf
            bias[0, j] = jnp.where(valid0, b, NEG)
            bias[1, j] = jnp.where(valid, b, NEG)

    def head_norm(x, w):
        x2 = x * x
        hi = x2.astype(BF16)
        lo = (x2 - hi.astype(F32)).astype(BF16)
        ss = jnp.dot(hi, same_head, preferred_element_type=F32) + jnp.dot(lo, same_head, preferred_element_type=F32)
        return x * lax.rsqrt(ss * (1.0 / HEAD_DIM) + EPS) * w

    def both_halves(x):
        swapped = pltpu.roll(x, HEAD_DIM, 1)
        return jnp.where(lane // HEAD_DIM == p // 2, x, swapped)

    def put_q(qs, rows, x):
        qs[0, rows, :] = jnp.where(low, x, 0.0).astype(BF16)
        qs[1, rows, :] = jnp.where(low, 0.0, x).astype(BF16)

    def put_v(vs, r, rows, x):
        vs[0, r, rows, :] = jnp.where(low, x, 1.0).astype(BF16)
        vs[1, r, rows, :] = jnp.where(low, 1.0, x).astype(BF16)

    qw, kw = qw_ref[...], kw_ref[...]
    qs1, ks1, vs1, _ = per[0]

    def prep(c, _):
        r0 = pl.multiple_of(c * ATT_PREP, ATT_PREP)
        rows = pl.ds(r0, ATT_PREP)
        krows = pl.ds(pl.multiple_of(BLOCK + r0, BLOCK), ATT_PREP)
        qn = head_norm(q_ref[0, rows, :], qw) * (HEAD_DIM ** -0.5)
        xk, xv = k_ref[0, rows, :], v_ref[0, rows, :]
        if dup_kv:
            xk, xv = both_halves(xk), both_halves(xv)
        kn = head_norm(xk, kw)
        if need32:
            qn_s[rows, :] = qn
            kn_s[rows, :] = kn
        put_q(qs1, rows, qn)
        ks1[0, krows, :] = kn.astype(BF16)
        put_v(vs1, 0, krows, xv)
        return 0

    lax.fori_loop(0, span // ATT_PREP, prep, 0)
    for (d, _), (qs, ks, vs, _) in zip(branches[1:], per[1:]):
        n_d = span // d
        for r in range(d):
            sub = pl.ds(r, n_d, stride=d)
            put_q(qs, slice(r * n_d, (r + 1) * n_d), qn_s[sub, :])
            ks[r, BLOCK:BLOCK + n_d, :] = kn_s[sub, :].astype(BF16)
            put_v(vs, r, slice(BLOCK, BLOCK + n_d), v_ref[0, sub, :])

    def process(bi, combos):
        d, _ = branches[bi]
        qs, ks, vs, bias = per[bi]
        n_d = span // d
        qv, kv, vv, bs, out_rows = [], [], [], [], []
        for r, n in combos:
            qrow = _static_or_multiple(r * n_d + n * BLOCK, BLOCK)
            krow = _static_or_multiple(n * BLOCK, BLOCK)
            bidx = jnp.where(jnp.logical_and(t == 0, n == 0), 0, 1)
            for j in range(2):
                qv.append(qs[j, pl.ds(qrow, BLOCK), :])
                kv.append(ks[r, pl.ds(krow, 2 * BLOCK), :])
                vv.append(vs[j, r, pl.ds(krow, 2 * BLOCK), :])
                bs.append(bias[bidx, j])
            out_rows.append(pl.ds(krow, BLOCK) if d == 1 else pl.ds(r + d * BLOCK * n, BLOCK, stride=d))
        s = [lax.dot_general(q, k, (((1,), (1,)), ((), ())), preferred_element_type=F32) + b
             for q, k, b in zip(qv, kv, bs)]
        m = [jnp.max(x, axis=-1, keepdims=True) for x in s]
        pr = [jnp.exp(x - mm).astype(BF16) for x, mm in zip(s, m)]
        acc = [jnp.dot(x, v, preferred_element_type=F32) for x, v in zip(pr, vv)]
        for ci, rows in enumerate(out_rows):
            a0, a1, m0, m1 = acc[2 * ci], acc[2 * ci + 1], m[2 * ci], m[2 * ci + 1]
            num = jnp.where(low, a0, a1)
            den = pltpu.roll(jnp.where(low, a1, a0), HEAD_DIM, 1)
            mx = jnp.where(low, m0, m1)
            if nbr == 1:
                if with_sinks:
                    sk = jnp.where(low, sinks_ref[2 * p], sinks_ref[2 * p + 1])
                    den = den + jnp.exp(sk - mx)
                o_ref[0, rows, :] = (num / den).astype(o_ref.dtype)
            elif bi == 0:
                acc_s[rows, :] = num
                l_s[rows, :] = den
                m_s[rows, :] = mx
            else:
                m_old = m_s[rows, :]
                m_new = jnp.maximum(m_old, mx)
                w_old, w_new = jnp.exp(m_old - m_new), jnp.exp(mx - m_new)
                num = w_old * acc_s[rows, :] + w_new * num
                den = w_old * l_s[rows, :] + w_new * den
                if bi == nbr - 1:
                    acc_s[rows, :] = num / den
                else:
                    acc_s[rows, :] = num
                    l_s[rows, :] = den
                    m_s[rows, :] = m_new

    for bi, (d, _) in enumerate(branches):
        nblk = span // d // BLOCK

        def group(g, _, bi=bi, nblk=nblk):
            if nblk >= ATT_GROUP:
                base = g * ATT_GROUP
                r = base // nblk
                combos = [(r, base % nblk + i) for i in range(ATT_GROUP)]
            else:
                per_grp = ATT_GROUP // nblk
                combos = [(g * per_grp + i // nblk, i % nblk) for i in range(ATT_GROUP)]
            process(bi, combos)
            return 0

        lax.fori_loop(0, span // BLOCK // ATT_GROUP, group, 0)
    if nbr > 1:
        o_ref[0] = acc_s[...].astype(o_ref.dtype)


def _attention_call(qkv, slopes, sinks, qw, kw, *, branches, qblk, kblk, vblk, shared_kv, head0, with_sinks, name):
    bsz, seq, _ = qkv.shape
    span = min(seq, ATT_SPAN)
    npair = 4
    kmap = (lambda b, p, t: (b, t, kblk)) if shared_kv else (lambda b, p, t: (b, t, kblk + p))
    vmap = (lambda b, p, t: (b, t, vblk)) if shared_kv else (lambda b, p, t: (b, t, vblk + p))
    smem = pl.BlockSpec(memory_space=pltpu.SMEM)
    scratch = []
    if len(branches) > 1:
        scratch += [pltpu.VMEM((span, LANE), F32), pltpu.VMEM((span, LANE), F32)]
    for d, _ in branches:
        n_d = span // d
        scratch += [pltpu.VMEM((2, span, LANE), BF16), pltpu.VMEM((d, BLOCK + n_d, LANE), BF16),
                    pltpu.VMEM((2, d, BLOCK + n_d, LANE), BF16), pltpu.VMEM((2, 2, BLOCK, 2 * BLOCK), F32)]
    if len(branches) > 1:
        scratch += [pltpu.VMEM((span, LANE), F32)] * 3
    tile2 = lambda w: jnp.tile(w.reshape(1, HEAD_DIM), (1, 2))
    return pl.pallas_call(
        functools.partial(_attn_kernel, branches=branches, head0=head0, with_sinks=with_sinks, dup_kv=shared_kv),
        grid=(bsz, npair, seq // span),
        in_specs=[smem, smem,
                  pl.BlockSpec((1, span, LANE), lambda b, p, t: (b, t, qblk + p)),
                  pl.BlockSpec((1, span, LANE), kmap),
                  pl.BlockSpec((1, span, LANE), vmap),
                  _resident((1, LANE)), _resident((1, LANE))],
        out_specs=pl.BlockSpec((1, span, LANE), lambda b, p, t: (b, t, p)),
        out_shape=jax.ShapeDtypeStruct((bsz, seq, npair * LANE), BF16),
        scratch_shapes=scratch,
        compiler_params=_cparams(("parallel", "parallel", "arbitrary")),
        name=name,
    )(slopes, sinks, qkv, qkv, qkv, tile2(qw), tile2(kw))


def attention(qkv, slopes, sinks, qw_a, kw_a, qw_b, kw_b):
    oa = _attention_call(qkv, slopes, sinks, qw_a, kw_a, branches=((1, A_WINDOW - 1),), qblk=0, kblk=4, vblk=5,
                         shared_kv=True, head0=0, with_sinks=True, name="attn_a")
    ob = _attention_call(qkv, slopes, sinks, qw_b, kw_b, branches=tuple((d, w // d) for w, d in B_BRANCHES),
                         qblk=6, kblk=10, vblk=14, shared_kv=False, head0=A_Q_HEADS, with_sinks=False, name="attn_b")
    return oa, ob


def _mix_out_kernel(x_ref, g_ref, ya_ref, yb_ref, w_ref, out_ref):
    na = ya_ref.shape[-1]
    y = jnp.dot(ya_ref[...], w_ref[:na, :], preferred_element_type=F32)
    y = y + jnp.dot(yb_ref[...], w_ref[na:, :], preferred_element_type=F32)
    out_ref[...] = x_ref[...] + g_ref[0] * y


def mix_out(x2, seq, gate, ya, yb, w, name, tm=512):
    n, d = x2.shape
    per = seq // tm
    row = lambda i: (i, 0)
    return pl.pallas_call(
        _mix_out_kernel,
        grid=(n // tm,),
        in_specs=[pl.BlockSpec((tm, d), row), pl.BlockSpec((1, 1, d), lambda i: (i // per, 0, 0)),
                  pl.BlockSpec((tm, ya.shape[-1]), row), pl.BlockSpec((tm, yb.shape[-1]), row), _resident(w.shape)],
        out_specs=pl.BlockSpec((tm, d), row),
        out_shape=jax.ShapeDtypeStruct((n, d), F32),
        compiler_params=_cparams(("parallel",)),
        name=name,
    )(x2, gate, ya, yb, w)


def _gelu_tanh(x):
    return 0.5 * x * (1.0 + jnp.tanh(math.sqrt(2.0 / math.pi) * (x + 0.044715 * (x * x * x))))


def _s5_kernel(u_ref, e_ref, pr_ref, pi_ref, cr_ref, ci_ref, d_ref, gw_ref, gb_ref, o_ref, xr_s, xi_s, car_s):
    t = pl.program_id(1)
    tc = u_ref.shape[1]
    ns = pr_ref.shape[1]

    @pl.when(t == 0)
    def _():
        car_s[...] = jnp.zeros_like(car_s)

    u = u_ref[0]
    e = jnp.dot(u.astype(BF16), e_ref[...], preferred_element_type=F32)
    xr, xi = e[:, :ns], e[:, ns:]
    xr = xr.reshape(tc // 8, 8, ns)
    xi = xi.reshape(tc // 8, 8, ns)
    row = lax.broadcasted_iota(jnp.int32, (tc // 8, 8, ns), 1)
    for s in (1, 2, 4):
        ar, ai = pr_ref[s - 1:s, :], pi_ref[s - 1:s, :]
        keep = row >= s
        sr = jnp.where(keep, pltpu.roll(xr, s, 1), 0.0)
        si = jnp.where(keep, pltpu.roll(xi, s, 1), 0.0)
        xr, xi = xr + (ar * sr - ai * si), xi + (ar * si + ai * sr)
    xr_s[...] = xr.reshape(tc, ns)
    xi_s[...] = xi.reshape(tc, ns)
    pr, pi = pr_ref[...], pi_ref[...]

    def grp(k, carry):
        cr, ci = carry
        r0 = pl.multiple_of(k * 8, 8)
        nr = xr_s[pl.ds(r0, 8), :] + (pr * cr - pi * ci)
        ni = xi_s[pl.ds(r0, 8), :] + (pr * ci + pi * cr)
        xr_s[pl.ds(r0, 8), :] = nr
        xi_s[pl.ds(r0, 8), :] = ni
        return nr[7:8, :], ni[7:8, :]

    cr, ci = lax.fori_loop(0, tc // 8, grp, (car_s[0:1, :], car_s[1:2, :]))
    car_s[0:1, :] = cr
    car_s[1:2, :] = ci
    y = (jnp.dot(xr_s[...].astype(BF16), cr_ref[...], preferred_element_type=F32)
         - jnp.dot(xi_s[...].astype(BF16), ci_ref[...], preferred_element_type=F32)
         + d_ref[...] * u)
    g = _gelu_tanh(y)
    z = jnp.dot(g.astype(BF16), gw_ref[...], preferred_element_type=F32) + gb_ref[...]
    o_ref[0] = (g * _sigmoid(z)).astype(o_ref.dtype)


def s5_mixer(rec, lam_re, lam_im, log_dt, b_re, b_im, c_re, c_im, d_skip, glu_w, glu_b, ucol=0, tc=512):
    bsz, seq, _ = rec.shape
    ublk = ucol // S5_WIDTH
    ns = S5_GROUPS * S5_STATE
    dt = jnp.exp(log_dt)[:, None]
    lr, li = lam_re, lam_im
    mag, ang = jnp.exp(lr * dt), li * dt
    ab_re, ab_im = mag * jnp.cos(ang), mag * jnp.sin(ang)
    nr, ni = ab_re - 1.0, ab_im
    den = lr * lr + li * li
    f_re = (nr * lr + ni * li) / den
    f_im = (ni * lr - nr * li) / den
    kk = jnp.arange(1, 9, dtype=F32)[:, None, None]
    pmag = jnp.exp(kk * (lr * dt)[None])
    p_re = (pmag * jnp.cos(kk * ang[None])).reshape(8, ns)
    p_im = (pmag * jnp.sin(kk * ang[None])).reshape(8, ns)
    e_re = f_re[..., None] * b_re - f_im[..., None] * b_im
    e_im = f_re[..., None] * b_im + f_im[..., None] * b_re
    eye = jnp.eye(S5_GROUPS, dtype=F32)

    def bd_in(m):
        return jnp.einsum("gpi,gh->gihp", m, eye).reshape(S5_WIDTH, ns)

    def bd_out(m):
        return jnp.einsum("gip,gh->gphi", m, eye).reshape(ns, S5_WIDTH)

    e_mat = jnp.concatenate([bd_in(e_re), bd_in(e_im)], axis=1).astype(BF16)
    return pl.pallas_call(
        _s5_kernel,
        grid=(bsz, seq // tc),
        in_specs=[pl.BlockSpec((1, tc, S5_WIDTH), lambda b, t: (b, t, ublk)),
                  _resident((S5_WIDTH, 2 * ns)), _resident((8, ns)), _resident((8, ns)),
                  _resident((ns, S5_WIDTH)), _resident((ns, S5_WIDTH)),
                  _resident((1, S5_WIDTH)), _resident((S5_WIDTH, S5_WIDTH)), _resident((1, S5_WIDTH))],
        out_specs=pl.BlockSpec((1, tc, S5_WIDTH), lambda b, t: (b, t, 0)),
        out_shape=jax.ShapeDtypeStruct((bsz, seq, S5_WIDTH), BF16),
        scratch_shapes=[pltpu.VMEM((tc, ns), F32), pltpu.VMEM((tc, ns), F32), pltpu.VMEM((8, ns), F32)],
        compiler_params=_cparams(("parallel", "arbitrary")),
        name="s5",
    )(rec, e_mat, p_re, p_im, bd_out(c_re).astype(BF16), bd_out(c_im).astype(BF16),
      d_skip.reshape(1, S5_WIDTH), glu_w.astype(BF16), glu_b.reshape(1, S5_WIDTH))


def _dot_bf(a, b):
    return jnp.dot(a.astype(BF16), b.astype(BF16), preferred_element_type=F32)


def _dot_nt_bf(a, b):
    return lax.dot_general(a.astype(BF16), b.astype(BF16), (((1,), (1,)), ((), ())), preferred_element_type=F32)


def _dot_tn_bf(a, b):
    return lax.dot_general(a.astype(BF16), b.astype(BF16), (((0,), (0,)), ((), ())), preferred_element_type=F32)


def _split_bf(x):
    hi = x.astype(BF16)
    return hi, (x - hi.astype(F32)).astype(BF16)


def _dot_x3(a, b):
    ah, al = _split_bf(a)
    bh, bl = _split_bf(b)
    return (jnp.dot(ah, bh, preferred_element_type=F32) + jnp.dot(ah, bl, preferred_element_type=F32)
            + jnp.dot(al, bh, preferred_element_type=F32))


def _dot_01(m01, x):
    hi = x.astype(BF16)
    r = x - hi.astype(F32)
    mid = r.astype(BF16)
    lo = (r - mid.astype(F32)).astype(BF16)
    return (jnp.dot(m01, hi, preferred_element_type=F32) + jnp.dot(m01, mid, preferred_element_type=F32)
            + jnp.dot(m01, lo, preferred_element_type=F32))


def _dn_kernel(q_ref, k_ref, v_ref, qh_ref, kh_ref, vh_ref, z_ref, ab_ref, wq_ref, wk_ref, wv_ref, nw_ref,
               alog_ref, dtb_ref, o_ref, s_s):
    t = pl.program_id(1)
    tc = q_ref.shape[1]
    c = DN_CHUNK

    @pl.when(t == 0)
    def _():
        s_s[...] = jnp.zeros_like(s_s)

    live = (t > 0).astype(F32)

    def conv_silu(x_ref, halo_ref, w_ref):
        xe = jnp.concatenate([halo_ref[0] * live, x_ref[0]], axis=0)
        w = w_ref[...]
        y = w[DN_CONV - 1:DN_CONV, :] * xe[8:, :]
        for s in range(1, DN_CONV):
            y = y + w[DN_CONV - 1 - s:DN_CONV - s, :] * pltpu.roll(xe, s, 0)[8:, :]
        return _silu(y)

    def head_l2(x2, scale):
        out = []
        for j in range(DN_HEADS):
            x = x2[:, j * DN_DK:(j + 1) * DN_DK]
            out.append(x * lax.rsqrt(jnp.sum(x * x, axis=-1, keepdims=True) + EPS) * scale)
        return out

    qn = head_l2(conv_silu(q_ref, qh_ref, wq_ref), DN_DK ** -0.5)
    kn = head_l2(conv_silu(k_ref, kh_ref, wk_ref), 1.0)
    v2 = conv_silu(v_ref, vh_ref, wv_ref)
    ab = ab_ref[0]
    bfull = _sigmoid(ab)
    xa = ab + dtb_ref[...]
    gfull = -jnp.exp(alog_ref[...]) * (jnp.maximum(xa, 0.0) + jnp.log(1.0 + jnp.exp(-jnp.abs(xa))))

    ri = lax.broadcasted_iota(jnp.int32, (c, c), 0)
    ci = lax.broadcasted_iota(jnp.int32, (c, c), 1)
    causal = ri >= ci
    strict = ri > ci
    tril = causal.astype(BF16)
    triu = (ri <= ci).astype(F32)
    ones = jnp.ones((c, c), BF16)

    nck = tc // c
    probs = [(i, j) for i in range(nck) for j in range(DN_HEADS)]
    rows = lambda x, i: x[i * c:(i + 1) * c]
    qc = [rows(qn[j], i) for i, j in probs]
    kc = [rows(kn[j], i) for i, j in probs]
    vc = [rows(v2[:, j * DN_DK:(j + 1) * DN_DK], i) for i, j in probs]
    bc = [rows(bfull[:, DN_HEADS + j:DN_HEADS + j + 1], i) for i, j in probs]
    gb = [jnp.broadcast_to(rows(gfull[:, j:j + 1], i), (c, c)) for i, j in probs]
    cs_l = jnp.concatenate([tril, -ones], axis=1)
    zero = jnp.zeros((c, c), F32)
    cs = [_dot_01(cs_l, jnp.concatenate([jnp.concatenate([x, x], axis=1),
                                         jnp.concatenate([x * triu, zero], axis=1)], axis=0)) for x in gb]
    gdiff = [x[:, :c] for x in cs]
    gi = [x[:, c:c + 1] for x in cs]
    glast = [x[c - 1:c, c:c + 1] for x in cs]
    qkk = [_dot_nt_bf(jnp.concatenate([x, y], axis=0), y) for x, y in zip(qc, kc)]
    gamma = [jnp.where(causal, jnp.exp(jnp.where(causal, x, 0.0)), 0.0) for x in gdiff]
    nm = [jnp.where(strict, b * x[c:] * gm, 0.0) for b, x, gm in zip(bc, qkk, gamma)]
    qk = [x[:c] for x in qkk]
    eye = (ri == ci).astype(F32)
    inv = [eye - x for x in nm]
    pw = [_dot_x3(x, x) for x in nm]
    span = 2
    while span < c:
        inv = [x + _dot_x3(x, y) for x, y in zip(inv, pw)]
        span *= 2
        if span < c:
            pw = [_dot_x3(y, y) for y in pw]
    eg = [jnp.exp(x) for x in gi]
    qk = [x * gm for x, gm in zip(qk, gamma)]
    kd = [k * jnp.exp(gl - g) for k, gl, g in zip(kc, glast, gi)]
    w2 = [jnp.concatenate([_dot_tn_bf(x, t), _dot_bf(y, t)], axis=0).astype(BF16) for x, y, t in zip(kd, qk, inv)]
    ke = [k * e for k, e in zip(kc, eg)]
    ke_hi = [x.astype(BF16) for x in ke]
    ke_lo = [(x - h.astype(F32)).astype(BF16) for x, h in zip(ke, ke_hi)]
    l1 = [jnp.concatenate([h, lo, (q * e).astype(BF16)], axis=0) for h, lo, q, e in zip(ke_hi, ke_lo, qc, eg)]
    decay = [jnp.exp(x) for x in glast]

    nw = nw_ref[...]
    states = [s_s[j] for j in range(DN_HEADS)]
    for i in range(nck):
        sl = slice(i * c, (i + 1) * c)
        ns = [i * DN_HEADS + j for j in range(DN_HEADS)]
        s_hi = [s.astype(BF16) for s in states]
        s_lo = [(s - h.astype(F32)).astype(BF16) for s, h in zip(states, s_hi)]
        p1 = [jnp.dot(l1[n], h, preferred_element_type=F32) for n, h in zip(ns, s_hi)]
        p2 = [jnp.dot(ke_hi[n], lo, preferred_element_type=F32) for n, lo in zip(ns, s_lo)]
        res = [bc[n] * (vc[n] - (a[:c] + a[c:2 * c] + b)) for n, a, b in zip(ns, p1, p2)]
        p3 = [jnp.dot(w2[n], r.astype(BF16), preferred_element_type=F32) for n, r in zip(ns, res)]
        outs = []
        for j in range(DN_HEADS):
            o = p1[j][2 * c:] + p3[j][DN_DK:]
            states[j] = states[j] * decay[ns[j]] + p3[j][:DN_DK]
            outs.append(o * lax.rsqrt(jnp.mean(o * o, axis=-1, keepdims=True) + EPS) * nw)
        o_ref[0, sl, :] = (jnp.concatenate(outs, axis=-1) * _silu(z_ref[0, sl, :])).astype(o_ref.dtype)
    for j in range(DN_HEADS):
        s_s[j] = states[j]


def deltanet_mixer(rec, conv_w, a_log, dt_bias, out_norm, cols, tc=256):
    bsz, seq, _ = rec.shape
    cq, ck, cv, cz, cab = cols
    hb = tc // 8
    dq = DN_HEADS * DN_DK

    def col(c0):
        return pl.BlockSpec((1, tc, dq), lambda b, t: (b, t, c0 // dq))

    def halo(c0):
        return pl.BlockSpec((1, 8, dq), lambda b, t: (b, jnp.maximum(t * hb - 1, 0), c0 // dq))

    def wcol(c0):
        return pl.BlockSpec((DN_CONV, dq), lambda b, t: (0, c0 // dq))

    def lanes(x):
        return jnp.zeros((1, LANE), F32).at[0, :DN_HEADS].set(x)

    return pl.pallas_call(
        _dn_kernel,
        grid=(bsz, seq // tc),
        in_specs=[col(cq), col(ck), col(cv), halo(cq), halo(ck), halo(cv), col(cz),
                  pl.BlockSpec((1, tc, LANE), lambda b, t: (b, t, cab // LANE)),
                  wcol(0), wcol(dq), wcol(2 * dq), _resident((1, DN_DK)), _resident((1, LANE)), _resident((1, LANE))],
        out_specs=pl.BlockSpec((1, tc, dq), lambda b, t: (b, t, 0)),
        out_shape=jax.ShapeDtypeStruct((bsz, seq, dq), BF16),
        scratch_shapes=[pltpu.VMEM((DN_HEADS, DN_DK, DN_DK), F32)],
        compiler_params=_cparams(("parallel", "arbitrary")),
        name="deltanet",
    )(rec, rec, rec, rec, rec, rec, rec, rec, conv_w, conv_w, conv_w, out_norm.reshape(1, DN_DK),
      lanes(a_log), lanes(dt_bias))


FFN_HALO = 16
FFN_AHEAD = 3


def _ffn_kernel(x_ref, xh_ref, nw_ref, sh_ref, sc_ref, g_ref, wu_ref, cw_ref, wd_ref, out_ref, hb_s, up_s, *, per):
    i = pl.program_id(0)
    tm = x_ref.shape[0]
    nf = wd_ref.shape[0]
    nw, sh, sc = nw_ref[...], sh_ref[0], sc_ref[0]
    x = x_ref[...]
    live = (i % per > 0).astype(F32)
    hb_s[0:FFN_HALO, :] = (_modulate(xh_ref[...], nw, sc, sh) * live).astype(BF16)
    hb_s[FFN_HALO:, :] = _modulate(x, nw, sc, sh).astype(BF16)
    hb = hb_s[...]

    def conv(slot, half, w):
        y = w[FFN_CONV - 1:FFN_CONV, :] * up_s[slot, half, FFN_HALO:, :]
        for s in range(1, FFN_CONV):
            y = y + w[FFN_CONV - 1 - s:FFN_CONV - s, :] * up_s[slot, half, FFN_HALO - s:FFN_HALO - s + tm, :]
        return y

    nslot, cw = up_s.shape[0], up_s.shape[-1]
    cols = lambda half, f: slice(half * nf * cw + f * cw, half * nf * cw + (f + 1) * cw)

    def up(f):
        for half in range(2):
            up_s[f % nslot, half] = jnp.dot(hb, wu_ref[:, cols(half, f)], preferred_element_type=F32)

    acc = jnp.zeros((tm, out_ref.shape[-1]), F32)
    for f in range(min(FFN_AHEAD, nf)):
        up(f)
    for f in range(nf):
        if f + FFN_AHEAD < nf:
            up(f + FFN_AHEAD)
        act = _silu(conv(f % nslot, 0, cw_ref[:, cols(0, f)])) * conv(f % nslot, 1, cw_ref[:, cols(1, f)])
        acc = acc + jnp.dot(act.astype(BF16), wd_ref[f], preferred_element_type=F32)
    out_ref[...] = x + g_ref[0] * acc


def conv_ffn(x2, seq, nw, shift, scale, gate, w_up, conv_w, w_down, tm=512, cw=256):
    n, d = x2.shape
    dff = w_down.shape[0]
    nf = dff // cw
    per = seq // tm
    wu = w_up.astype(BF16)
    cwt = conv_w
    wd = w_down.astype(BF16).reshape(nf, cw, d)
    row = lambda i: (i, 0)
    bat = lambda i: (i // per, 0, 0)
    hpt = tm // FFN_HALO
    return pl.pallas_call(
        functools.partial(_ffn_kernel, per=per),
        grid=(n // tm,),
        in_specs=[pl.BlockSpec((tm, d), row),
                  pl.BlockSpec((FFN_HALO, d), lambda i: (jnp.maximum(i * hpt - 1, 0), 0)),
                  _resident((1, d)), pl.BlockSpec((1, 1, d), bat), pl.BlockSpec((1, 1, d), bat),
                  pl.BlockSpec((1, 1, d), bat),
                  _resident(wu.shape), _resident(cwt.shape), _resident(wd.shape)],
        out_specs=pl.BlockSpec((tm, d), row),
        out_shape=jax.ShapeDtypeStruct((n, d), F32),
        scratch_shapes=[pltpu.VMEM((FFN_HALO + tm, d), BF16),
                        pltpu.VMEM((FFN_AHEAD + 1, 2, FFN_HALO + tm, cw), F32)],
        compiler_params=_cparams(("parallel",)),
        name="ffn",
    )(x2, x2, nw.reshape(1, d), shift, scale, gate, wu, cwt, wd)


def _alibi_slopes(n):
    return jnp.asarray(2.0 ** (-8.0 * np.arange(1, n + 1) / n), dtype=F32)


def _pad_cols(w, mult):
    pad = (-w.shape[1]) % mult
    return jnp.pad(w, ((0, 0), (0, pad)))


def _rec_weight_layout(w):
    dq = DN_HEADS * DN_DK
    nu = S5_WIDTH
    w2 = _pad_cols(jnp.concatenate([w[:, nu:nu + 4 * dq], w[:, :nu], w[:, nu + 4 * dq:]], axis=1), LANE)
    cols = (0, dq, 2 * dq, 3 * dq, 4 * dq + nu)
    return w2, cols, 4 * dq


def kernel(x, c, ada_w, ada_b, norm_mix, norm_ffn, attn_w_in, attn_q_norm_a, attn_k_norm_a, attn_q_norm_b,
           attn_k_norm_b, attn_sinks, attn_w_out, rec_w_in, s5_lambda_re, s5_lambda_im, s5_log_dt, s5_b_re,
           s5_b_im, s5_c_re, s5_c_im, s5_d, s5_glu_w, s5_glu_b, dn_conv, dn_a_log, dn_dt_bias, dn_out_norm,
           rec_w_out, ffn_w_up, ffn_conv, ffn_w_down):
    bsz, seq, d = x.shape
    depth = ada_w.shape[0]
    mod = adaln(c, ada_w, ada_b)
    x2 = x.reshape(bsz * seq, d)
    slopes = _alibi_slopes(N_ATTN_HEADS)
    for layer in range(depth):
        sh1, sc1, g1, sh2, sc2, g2 = [mod[layer, :, j * d:(j + 1) * d].reshape(bsz, 1, d) for j in range(6)]
        i = layer // 2
        if layer % 2 == 0:
            qkv = inproj(x2, seq, norm_mix[layer], sh1, sc1, attn_w_in[i].astype(BF16))
            qkv = qkv.reshape(bsz, seq, -1)
            oa, ob = attention(qkv, slopes, attn_sinks[i], attn_q_norm_a[i], attn_k_norm_a[i],
                               attn_q_norm_b[i], attn_k_norm_b[i])
            x2 = mix_out(x2, seq, g1, oa.reshape(bsz * seq, -1), ob.reshape(bsz * seq, -1),
                         attn_w_out[i].astype(BF16), "attn_out")
        else:
            w_rec, rec_cols, ucol = _rec_weight_layout(rec_w_in[i])
            rec = inproj(x2, seq, norm_mix[layer], sh1, sc1, w_rec.astype(BF16))
            rec = rec.reshape(bsz, seq, -1)
            yc = s5_mixer(rec, s5_lambda_re[i], s5_lambda_im[i], s5_log_dt[i], s5_b_re[i], s5_b_im[i],
                          s5_c_re[i], s5_c_im[i], s5_d[i], s5_glu_w[i], s5_glu_b[i], ucol=ucol)
            yd = deltanet_mixer(rec, dn_conv[i], dn_a_log[i], dn_dt_bias[i], dn_out_norm[i], rec_cols)
            x2 = mix_out(x2, seq, g1, yc.reshape(bsz * seq, -1), yd.reshape(bsz * seq, -1),
                         rec_w_out[i].astype(BF16), "rec_out")
        x2 = conv_ffn(x2, seq, norm_ffn[layer], sh2, sc2, g2, ffn_w_up[layer], ffn_conv[layer], ffn_w_down[layer])
    return x2.reshape(bsz, seq, d)
```

```python
import functools
import math

import numpy as np
import jax
import jax.numpy as jnp
from jax import lax
from jax.experimental import pallas as pl
from jax.experimental.pallas import tpu as pltpu

F32 = jnp.float32
BF16 = jnp.bfloat16
HIGHEST = lax.Precision.HIGHEST
EPS = 1e-6
NEG = -1e30

HEAD_DIM = 64
BLOCK = 128
A_Q_HEADS = 8
A_KV_HEADS = 2
A_WINDOW = 128
B_HEADS = 8
B_BRANCHES = ((128, 1), (512, 4), (2048, 16))
N_ATTN_HEADS = A_Q_HEADS + B_HEADS
S5_GROUP = 16
S5_GROUPS = 16
S5_WIDTH = 256
S5_STATE = 64
DN_HEADS = 6
DN_DK = 128
DN_CONV = 4
DN_CHUNK = 64
FFN_CONV = 3

V7X_VMEM_LIMIT = 60 * 1024 * 1024
LANE = 128


def _cparams(sem, vmem=V7X_VMEM_LIMIT):
    return pltpu.CompilerParams(dimension_semantics=sem, vmem_limit_bytes=vmem)


def _resident(shape):
    n = len(shape)
    return pl.BlockSpec(shape, lambda *_: (0,) * n, pipeline_mode=pl.Buffered(1))


def _sigmoid(x):
    return 1.0 / (1.0 + jnp.exp(-x))


def _silu(x):
    h = 0.5 * x
    return h + h * jnp.tanh(h)


def _modulate(x, nw, scale, shift):
    ms = jnp.mean(x * x, axis=-1, keepdims=True)
    y = x * lax.rsqrt(ms + EPS) * nw
    return y * (1.0 + scale) + shift


def _adaln_kernel(c_ref, w_ref, b_ref, o_ref):
    c = c_ref[...]
    o_ref[0] = jnp.dot(_silu(c), w_ref[0], precision=HIGHEST, preferred_element_type=F32) + b_ref[0]


def adaln(c, ada_w, ada_b, tn=1536):
    depth, d, n = ada_w.shape
    bsz = c.shape[0]
    cp = jnp.zeros((8, d), F32).at[:bsz].set(c)
    out = pl.pallas_call(
        _adaln_kernel,
        grid=(depth, n // tn),
        in_specs=[pl.BlockSpec((8, d), lambda l, j: (0, 0)),
                  pl.BlockSpec((1, d, tn), lambda l, j: (l, 0, j)),
                  pl.BlockSpec((1, 1, tn), lambda l, j: (l, 0, j))],
        out_specs=pl.BlockSpec((1, 8, tn), lambda l, j: (l, 0, j)),
        out_shape=jax.ShapeDtypeStruct((depth, 8, n), F32),
        compiler_params=_cparams(("arbitrary", "arbitrary")),
        name="adaln",
    )(cp, ada_w, ada_b.reshape(depth, 1, n))
    return out[:, :bsz]


def _inproj_kernel(x_ref, nw_ref, sh_ref, sc_ref, w_ref, o_ref):
    h = _modulate(x_ref[...], nw_ref[...], sc_ref[0], sh_ref[0]).astype(BF16)
    o_ref[...] = jnp.dot(h, w_ref[...], preferred_element_type=F32)


def inproj(x2, seq, nw, shift, scale, w, tm=512):
    n, d = x2.shape
    nout = w.shape[1]
    per = seq // tm
    return pl.pallas_call(
        _inproj_kernel,
        grid=(n // tm,),
        in_specs=[pl.BlockSpec((tm, d), lambda i: (i, 0)),
                  _resident((1, d)),
                  pl.BlockSpec((1, 1, d), lambda i: (i // per, 0, 0)),
                  pl.BlockSpec((1, 1, d), lambda i: (i // per, 0, 0)),
                  _resident((d, nout))],
        out_specs=pl.BlockSpec((tm, nout), lambda i: (i, 0)),
        out_shape=jax.ShapeDtypeStruct((n, nout), F32),
        compiler_params=_cparams(("parallel",)),
        name="inproj",
    )(x2, nw.reshape(1, d), shift, scale, w)


ATT_SPAN = 2048
ATT_GROUP = 16
ATT_PREP = 512


def _static_or_multiple(x, k):
    return x if isinstance(x, int) else pl.multiple_of(x, k)


def _attn_kernel(slopes_ref, sinks_ref, q_ref, k_ref, v_ref, qw_ref, kw_ref, o_ref, *scratch,
                 branches, head0, with_sinks, dup_kv):
    p = pl.program_id(1)
    t = pl.program_id(2)
    span = q_ref.shape[1]
    nbr = len(branches)
    assert branches[0][0] == 1
    need32 = nbr > 1
    it = iter(scratch)
    qn_s, kn_s = (next(it), next(it)) if need32 else (None, None)
    per = [(next(it), next(it), next(it), next(it)) for _ in branches]
    acc_s, m_s, l_s = (next(it), next(it), next(it)) if nbr > 1 else (None, None, None)

    lane = lax.broadcasted_iota(jnp.int32, (1, LANE), 1)
    low = lane < HEAD_DIM
    li = lax.broadcasted_iota(jnp.int32, (LANE, LANE), 0) // HEAD_DIM
    lj = lax.broadcasted_iota(jnp.int32, (LANE, LANE), 1) // HEAD_DIM
    same_head = (li == lj).astype(BF16)

    for (d, _), (_, ks, vs, _) in zip(branches, per):
        n_d = span // d

        @pl.when(t == 0)
        def _():
            for r in range(d):
                ks[r, 0:BLOCK, :] = jnp.zeros((BLOCK, LANE), BF16)
                for j in range(2):
                    vs[j, r, 0:BLOCK, :] = jnp.zeros((BLOCK, LANE), BF16)

        @pl.when(t > 0)
        def _():
            for r in range(d):
                ks[r, 0:BLOCK, :] = ks[r, n_d:n_d + BLOCK, :]
                for j in range(2):
                    vs[j, r, 0:BLOCK, :] = vs[j, r, n_d:n_d + BLOCK, :]

    @pl.when(t == 0)
    def _():
        qi = lax.broadcasted_iota(jnp.int32, (BLOCK, 2 * BLOCK), 0)
        si = lax.broadcasted_iota(jnp.int32, (BLOCK, 2 * BLOCK), 1)
        dist = BLOCK + qi - si
        for (d, max_dist), (_, _, _, bias) in zip(branches, per):
            valid = (dist >= 0) & (dist <= max_dist)
            valid0 = valid & (si >= BLOCK)
            distf = (d * dist).astype(F32)
            for j in range(2):
                b = -slopes_ref[head0 + 2 * p + j] * distf
                bias[0, j] = jnp.where(valid0, b, NEG)
                bias[1, j] = jnp.where(valid, b, NEG)

    def head_norm(x, w):
        x2 = x * x
        hi = x2.astype(BF16)
        lo = (x2 - hi.astype(F32)).astype(BF16)
        ss = jnp.dot(hi, same_head, preferred_element_type=F32) + jnp.dot(lo, same_head, preferred_element_type=F32)
        return x * lax.rsqrt(ss * (1.0 / HEAD_DIM) + EPS) * w

    def both_halves(x):
        swapped = pltpu.roll(x, HEAD_DIM, 1)
        return jnp.where(lane // HEAD_DIM == p // 2, x, swapped)

    def put_q(qs, rows, x):
        qs[0, rows, :] = jnp.where(low, x, 0.0).astype(BF16)
        qs[1, rows, :] = jnp.where(low, 0.0, x).astype(BF16)

    def put_v(vs, r, rows, x):
        vs[0, r, rows, :] = jnp.where(low, x, 1.0).astype(BF16)
        vs[1, r, rows, :] = jnp.where(low, 1.0, x).astype(BF16)

    qw, kw = qw_ref[...], kw_ref[...]
    qs1, ks1, vs1, _ = per[0]

    def prep(c, _):
        r0 = pl.multiple_of(c * ATT_PREP, ATT_PREP)
        rows = pl.ds(r0, ATT_PREP)
        krows = pl.ds(pl.multiple_of(BLOCK + r0, BLOCK), ATT_PREP)
        qn = head_norm(q_ref[0, rows, :], qw) * (HEAD_DIM ** -0.5)
        xk, xv = k_ref[0, rows, :], v_ref[0, rows, :]
        if dup_kv:
            xk, xv = both_halves(xk), both_halves(xv)
        kn = head_norm(xk, kw)
        if need32:
            qn_s[rows, :] = qn
            kn_s[rows, :] = kn
        put_q(qs1, rows, qn)
        ks1[0, krows, :] = kn.astype(BF16)
        put_v(vs1, 0, krows, xv)
        return 0

    lax.fori_loop(0, span // ATT_PREP, prep, 0)
    for (d, _), (qs, ks, vs, _) in zip(branches[1:], per[1:]):
        n_d = span // d
        for r in range(d):
            sub = pl.ds(r, n_d, stride=d)
            put_q(qs, slice(r * n_d, (r + 1) * n_d), qn_s[sub, :])
            ks[r, BLOCK:BLOCK + n_d, :] = kn_s[sub, :].astype(BF16)
            put_v(vs, r, slice(BLOCK, BLOCK + n_d), v_ref[0, sub, :])

    def process(bi, combos):
        d, _ = branches[bi]
        qs, ks, vs, bias = per[bi]
        n_d = span // d
        qv, kv, vv, bs, out_rows = [], [], [], [], []
        for r, n in combos:
            qrow = _static_or_multiple(r * n_d + n * BLOCK, BLOCK)
            krow = _static_or_multiple(n * BLOCK, BLOCK)
            bidx = jnp.where(jnp.logical_and(t == 0, n == 0), 0, 1)
            for j in range(2):
                qv.append(qs[j, pl.ds(qrow, BLOCK), :])
                kv.append(ks[r, pl.ds(krow, 2 * BLOCK), :])
                vv.append(vs[j, r, pl.ds(krow, 2 * BLOCK), :])
                bs.append(bias[bidx, j])
            out_rows.append(pl.ds(krow, BLOCK) if d == 1 else pl.ds(r + d * BLOCK * n, BLOCK, stride=d))
        s = [lax.dot_general(q, k, (((1,), (1,)), ((), ())), preferred_element_type=F32) + b
             for q, k, b in zip(qv, kv, bs)]
        m = [jnp.max(x, axis=-1, keepdims=True) for x in s]
        pr = [jnp.exp(x - mm).astype(BF16) for x, mm in zip(s, m)]
        acc = [jnp.dot(x, v, preferred_element_type=F32) for x, v in zip(pr, vv)]
        for ci, rows in enumerate(out_rows):
            a0, a1, m0, m1 = acc[2 * ci], acc[2 * ci + 1], m[2 * ci], m[2 * ci + 1]
            num = jnp.where(low, a0, a1)
            den = pltpu.roll(jnp.where(low, a1, a0), HEAD_DIM, 1)
            mx = jnp.where(low, m0, m1)
            if nbr == 1:
                if with_sinks:
                    sk = jnp.where(low, sinks_ref[2 * p], sinks_ref[2 * p + 1])
                    den = den + jnp.exp(sk - mx)
                o_ref[0, rows, :] = (num / den).astype(o_ref.dtype)
            elif bi == 0:
                acc_s[rows, :] = num
                l_s[rows, :] = den
                m_s[rows, :] = mx
            else:
                m_old = m_s[rows, :]
                m_new = jnp.maximum(m_old, mx)
                w_old, w_new = jnp.exp(m_old - m_new), jnp.exp(mx - m_new)
                num = w_old * acc_s[rows, :] + w_new * num
                den = w_old * l_s[rows, :] + w_new * den
                if bi == nbr - 1:
                    acc_s[rows, :] = num / den
                else:
                    acc_s[rows, :] = num
                    l_s[rows, :] = den
                    m_s[rows, :] = m_new

    for bi, (d, _) in enumerate(branches):
        nblk = span // d // BLOCK

        def group(g, _, bi=bi, nblk=nblk):
            if nblk >= ATT_GROUP:
                base = g * ATT_GROUP
                r = base // nblk
                combos = [(r, base % nblk + i) for i in range(ATT_GROUP)]
            else:
                per_grp = ATT_GROUP // nblk
                combos = [(g * per_grp + i // nblk, i % nblk) for i in range(ATT_GROUP)]
            process(bi, combos)
            return 0

        lax.fori_loop(0, span // BLOCK // ATT_GROUP, group, 0)
    if nbr > 1:
        o_ref[0] = acc_s[...].astype(o_ref.dtype)


def _attention_call(qkv, slopes, sinks, qw, kw, *, branches, qblk, kblk, vblk, shared_kv, head0, with_sinks, name):
    bsz, seq, _ = qkv.shape
    span = min(seq, ATT_SPAN)
    npair = 4
    kmap = (lambda b, p, t: (b, t, kblk)) if shared_kv else (lambda b, p, t: (b, t, kblk + p))
    vmap = (lambda b, p, t: (b, t, vblk)) if shared_kv else (lambda b, p, t: (b, t, vblk + p))
    smem = pl.BlockSpec(memory_space=pltpu.SMEM)
    scratch = []
    if len(branches) > 1:
        scratch += [pltpu.VMEM((span, LANE), F32), pltpu.VMEM((span, LANE), F32)]
    for d, _ in branches:
        n_d = span // d
        scratch += [pltpu.VMEM((2, span, LANE), BF16), pltpu.VMEM((d, BLOCK + n_d, LANE), BF16),
                    pltpu.VMEM((2, d, BLOCK + n_d, LANE), BF16), pltpu.VMEM((2, 2, BLOCK, 2 * BLOCK), F32)]
    if len(branches) > 1:
        scratch += [pltpu.VMEM((span, LANE), F32)] * 3
    tile2 = lambda w: jnp.tile(w.reshape(1, HEAD_DIM), (1, 2))
    return pl.pallas_call(
        functools.partial(_attn_kernel, branches=branches, head0=head0, with_sinks=with_sinks, dup_kv=shared_kv),
        grid=(bsz, npair, seq // span),
        in_specs=[smem, smem,
                  pl.BlockSpec((1, span, LANE), lambda b, p, t: (b, t, qblk + p)),
                  pl.BlockSpec((1, span, LANE), kmap),
                  pl.BlockSpec((1, span, LANE), vmap),
                  _resident((1, LANE)), _resident((1, LANE))],
        out_specs=pl.BlockSpec((1, span, LANE), lambda b, p, t: (b, t, p)),
        out_shape=jax.ShapeDtypeStruct((bsz, seq, npair * LANE), BF16),
        scratch_shapes=scratch,
        compiler_params=_cparams(("parallel", "parallel", "arbitrary")),
        name=name,
    )(slopes, sinks, qkv, qkv, qkv, tile2(qw), tile2(kw))


def attention(qkv, slopes, sinks, qw_a, kw_a, qw_b, kw_b):
    oa = _attention_call(qkv, slopes, sinks, qw_a, kw_a, branches=((1, A_WINDOW - 1),), qblk=0, kblk=4, vblk=5,
                         shared_kv=True, head0=0, with_sinks=True, name="attn_a")
    ob = _attention_call(qkv, slopes, sinks, qw_b, kw_b, branches=tuple((d, w // d) for w, d in B_BRANCHES),
                         qblk=6, kblk=10, vblk=14, shared_kv=False, head0=A_Q_HEADS, with_sinks=False, name="attn_b")
    return oa, ob


def _mix_out_kernel(x_ref, g_ref, ya_ref, yb_ref, w_ref, out_ref):
    na = ya_ref.shape[-1]
    y = jnp.dot(ya_ref[...], w_ref[:na, :], preferred_element_type=F32)
    y = y + jnp.dot(yb_ref[...], w_ref[na:, :], preferred_element_type=F32)
    out_ref[...] = x_ref[...] + g_ref[0] * y


def mix_out(x2, seq, gate, ya, yb, w, name, tm=512):
    n, d = x2.shape
    per = seq // tm
    row = lambda i: (i, 0)
    return pl.pallas_call(
        _mix_out_kernel,
        grid=(n // tm,),
        in_specs=[pl.BlockSpec((tm, d), row), pl.BlockSpec((1, 1, d), lambda i: (i // per, 0, 0)),
                  pl.BlockSpec((tm, ya.shape[-1]), row), pl.BlockSpec((tm, yb.shape[-1]), row), _resident(w.shape)],
        out_specs=pl.BlockSpec((tm, d), row),
        out_shape=jax.ShapeDtypeStruct((n, d), F32),
        compiler_params=_cparams(("parallel",)),
        name=name,
    )(x2, gate, ya, yb, w)


def _gelu_tanh(x):
    return 0.5 * x * (1.0 + jnp.tanh(math.sqrt(2.0 / math.pi) * (x + 0.044715 * (x * x * x))))


def _s5_kernel(u_ref, e_ref, pr_ref, pi_ref, cr_ref, ci_ref, d_ref, gw_ref, gb_ref, o_ref, xr_s, xi_s, car_s):
    t = pl.program_id(1)
    tc = u_ref.shape[1]
    ns = pr_ref.shape[1]

    @pl.when(t == 0)
    def _():
        car_s[...] = jnp.zeros_like(car_s)

    u = u_ref[0]
    e = jnp.dot(u.astype(BF16), e_ref[...], preferred_element_type=F32)
    xr, xi = e[:, :ns], e[:, ns:]
    xr = xr.reshape(tc // 8, 8, ns)
    xi = xi.reshape(tc // 8, 8, ns)
    row = lax.broadcasted_iota(jnp.int32, (tc // 8, 8, ns), 1)
    for s in (1, 2, 4):
        ar, ai = pr_ref[s - 1:s, :], pi_ref[s - 1:s, :]
        keep = row >= s
        sr = jnp.where(keep, pltpu.roll(xr, s, 1), 0.0)
        si = jnp.where(keep, pltpu.roll(xi, s, 1), 0.0)
        xr, xi = xr + (ar * sr - ai * si), xi + (ar * si + ai * sr)
    xr_s[...] = xr.reshape(tc, ns)
    xi_s[...] = xi.reshape(tc, ns)
    pr, pi = pr_ref[...], pi_ref[...]

    def grp(k, carry):
        cr, ci = carry
        r0 = pl.multiple_of(k * 8, 8)
        nr = xr_s[pl.ds(r0, 8), :] + (pr * cr - pi * ci)
        ni = xi_s[pl.ds(r0, 8), :] + (pr * ci + pi * cr)
        xr_s[pl.ds(r0, 8), :] = nr
        xi_s[pl.ds(r0, 8), :] = ni
        return nr[7:8, :], ni[7:8, :]

    cr, ci = lax.fori_loop(0, tc // 8, grp, (car_s[0:1, :], car_s[1:2, :]))
    car_s[0:1, :] = cr
    car_s[1:2, :] = ci
    y = (jnp.dot(xr_s[...].astype(BF16), cr_ref[...], preferred_element_type=F32)
         - jnp.dot(xi_s[...].astype(BF16), ci_ref[...], preferred_element_type=F32)
         + d_ref[...] * u)
    g = _gelu_tanh(y)
    z = jnp.dot(g.astype(BF16), gw_ref[...], preferred_element_type=F32) + gb_ref[...]
    o_ref[0] = (g * _sigmoid(z)).astype(o_ref.dtype)


def s5_mixer(rec, lam_re, lam_im, log_dt, b_re, b_im, c_re, c_im, d_skip, glu_w, glu_b, ucol=0, tc=512):
    bsz, seq, _ = rec.shape
    ublk = ucol // S5_WIDTH
    ns = S5_GROUPS * S5_STATE
    dt = jnp.exp(log_dt)[:, None]
    lr, li = lam_re, lam_im
    mag, ang = jnp.exp(lr * dt), li * dt
    ab_re, ab_im = mag * jnp.cos(ang), mag * jnp.sin(ang)
    nr, ni = ab_re - 1.0, ab_im
    den = lr * lr + li * li
    f_re = (nr * lr + ni * li) / den
    f_im = (ni * lr - nr * li) / den
    kk = jnp.arange(1, 9, dtype=F32)[:, None, None]
    pmag = jnp.exp(kk * (lr * dt)[None])
    p_re = (pmag * jnp.cos(kk * ang[None])).reshape(8, ns)
    p_im = (pmag * jnp.sin(kk * ang[None])).reshape(8, ns)
    e_re = f_re[..., None] * b_re - f_im[..., None] * b_im
    e_im = f_re[..., None] * b_im + f_im[..., None] * b_re
    eye = jnp.eye(S5_GROUPS, dtype=F32)

    def bd_in(m):
        return jnp.einsum("gpi,gh->gihp", m, eye).reshape(S5_WIDTH, ns)

    def bd_out(m):
        return jnp.einsum("gip,gh->gphi", m, eye).reshape(ns, S5_WIDTH)

    e_mat = jnp.concatenate([bd_in(e_re), bd_in(e_im)], axis=1).astype(BF16)
    return pl.pallas_call(
        _s5_kernel,
        grid=(bsz, seq // tc),
        in_specs=[pl.BlockSpec((1, tc, S5_WIDTH), lambda b, t: (b, t, ublk)),
                  _resident((S5_WIDTH, 2 * ns)), _resident((8, ns)), _resident((8, ns)),
                  _resident((ns, S5_WIDTH)), _resident((ns, S5_WIDTH)),
                  _resident((1, S5_WIDTH)), _resident((S5_WIDTH, S5_WIDTH)), _resident((1, S5_WIDTH))],
        out_specs=pl.BlockSpec((1, tc, S5_WIDTH), lambda b, t: (b, t, 0)),
        out_shape=jax.ShapeDtypeStruct((bsz, seq, S5_WIDTH), BF16),
        scratch_shapes=[pltpu.VMEM((tc, ns), F32), pltpu.VMEM((tc, ns), F32), pltpu.VMEM((8, ns), F32)],
        compiler_params=_cparams(("parallel", "arbitrary")),
        name="s5",
    )(rec, e_mat, p_re, p_im, bd_out(c_re).astype(BF16), bd_out(c_im).astype(BF16),
      d_skip.reshape(1, S5_WIDTH), glu_w.astype(BF16), glu_b.reshape(1, S5_WIDTH))


def _dot_bf(a, b):
    return jnp.dot(a.astype(BF16), b.astype(BF16), preferred_element_type=F32)


def _dot_nt_bf(a, b):
    return lax.dot_general(a.astype(BF16), b.astype(BF16), (((1,), (1,)), ((), ())), preferred_element_type=F32)


def _dot_tn_bf(a, b):
    return lax.dot_general(a.astype(BF16), b.astype(BF16), (((0,), (0,)), ((), ())), preferred_element_type=F32)


def _split_bf(x):
    hi = x.astype(BF16)
    return hi, (x - hi.astype(F32)).astype(BF16)


def _dot_x3(a, b):
    ah, al = _split_bf(a)
    bh, bl = _split_bf(b)
    return (jnp.dot(ah, bh, preferred_element_type=F32) + jnp.dot(ah, bl, preferred_element_type=F32)
            + jnp.dot(al, bh, preferred_element_type=F32))


def _dot_01(m01, x):
    hi = x.astype(BF16)
    r = x - hi.astype(F32)
    mid = r.astype(BF16)
    lo = (r - mid.astype(F32)).astype(BF16)
    return (jnp.dot(m01, hi, preferred_element_type=F32) + jnp.dot(m01, mid, preferred_element_type=F32)
            + jnp.dot(m01, lo, preferred_element_type=F32))


def _dn_kernel(q_ref, k_ref, v_ref, qh_ref, kh_ref, vh_ref, z_ref, ab_ref, wq_ref, wk_ref, wv_ref, nw_ref,
               alog_ref, dtb_ref, o_ref, s_s):
    t = pl.program_id(1)
    tc = q_ref.shape[1]
    c = DN_CHUNK

    @pl.when(t == 0)
    def _():
        s_s[...] = jnp.zeros_like(s_s)

    live = (t > 0).astype(F32)

    def conv_silu(x_ref, halo_ref, w_ref):
        xe = jnp.concatenate([halo_ref[0] * live, x_ref[0]], axis=0)
        w = w_ref[...]
        y = w[DN_CONV - 1:DN_CONV, :] * xe[8:, :]
        for s in range(1, DN_CONV):
            y = y + w[DN_CONV - 1 - s:DN_CONV - s, :] * pltpu.roll(xe, s, 0)[8:, :]
        return _silu(y)

    def head_l2(x2, scale):
        out = []
        for j in range(DN_HEADS):
            x = x2[:, j * DN_DK:(j + 1) * DN_DK]
            out.append(x * lax.rsqrt(jnp.sum(x * x, axis=-1, keepdims=True) + EPS) * scale)
        return out

    qn = head_l2(conv_silu(q_ref, qh_ref, wq_ref), DN_DK ** -0.5)
    kn = head_l2(conv_silu(k_ref, kh_ref, wk_ref), 1.0)
    v2 = conv_silu(v_ref, vh_ref, wv_ref)
    ab = ab_ref[0]
    bfull = _sigmoid(ab)
    xa = ab + dtb_ref[...]
    gfull = -jnp.exp(alog_ref[...]) * (jnp.maximum(xa, 0.0) + jnp.log(1.0 + jnp.exp(-jnp.abs(xa))))

    ri = lax.broadcasted_iota(jnp.int32, (c, c), 0)
    ci = lax.broadcasted_iota(jnp.int32, (c, c), 1)
    causal = ri >= ci
    strict = ri > ci
    tril = causal.astype(BF16)
    triu = (ri <= ci).astype(F32)
    ones = jnp.ones((c, c), BF16)

    nck = tc // c
    probs = [(i, j) for i in range(nck) for j in range(DN_HEADS)]
    rows = lambda x, i: x[i * c:(i + 1) * c]
    qc = [rows(qn[j], i) for i, j in probs]
    kc = [rows(kn[j], i) for i, j in probs]
    vc = [rows(v2[:, j * DN_DK:(j + 1) * DN_DK], i) for i, j in probs]
    bc = [rows(bfull[:, DN_HEADS + j:DN_HEADS + j + 1], i) for i, j in probs]
    gb = [jnp.broadcast_to(rows(gfull[:, j:j + 1], i), (c, c)) for i, j in probs]
    cs_l = jnp.concatenate([tril, -ones], axis=1)
    zero = jnp.zeros((c, c), F32)
    cs = [_dot_01(cs_l, jnp.concatenate([jnp.concatenate([x, x], axis=1),
                                         jnp.concatenate([x * triu, zero], axis=1)], axis=0)) for x in gb]
    gdiff = [x[:, :c] for x in cs]
    gi = [x[:, c:c + 1] for x in cs]
    glast = [x[c - 1:c, c:c + 1] for x in cs]
    qkk = [_dot_nt_bf(jnp.concatenate([x, y], axis=0), y) for x, y in zip(qc, kc)]
    gamma = [jnp.where(causal, jnp.exp(jnp.where(causal, x, 0.0)), 0.0) for x in gdiff]
    nm = [jnp.where(strict, b * x[c:] * gm, 0.0) for b, x, gm in zip(bc, qkk, gamma)]
    qk = [x[:c] for x in qkk]
    eye = (ri == ci).astype(F32)
    inv = [eye - x for x in nm]
    pw = [_dot_x3(x, x) for x in nm]
    span = 2
    while span < c:
        inv = [x + _dot_x3(x, y) for x, y in zip(inv, pw)]
        span *= 2
        if span < c:
            pw = [_dot_x3(y, y) for y in pw]
    eg = [jnp.exp(x) for x in gi]
    qk = [x * gm for x, gm in zip(qk, gamma)]
    kd = [k * jnp.exp(gl - g) for k, gl, g in zip(kc, glast, gi)]
    w2 = [jnp.concatenate([_dot_tn_bf(x, t), _dot_bf(y, t)], axis=0).astype(BF16) for x, y, t in zip(kd, qk, inv)]
    ke = [k * e for k, e in zip(kc, eg)]
    ke_hi = [x.astype(BF16) for x in ke]
    ke_lo = [(x - h.astype(F32)).astype(BF16) for x, h in zip(ke, ke_hi)]
    l1 = [jnp.concatenate([h, lo, (q * e).astype(BF16)], axis=0) for h, lo, q, e in zip(ke_hi, ke_lo, qc, eg)]
    decay = [jnp.exp(x) for x in glast]

    nw = nw_ref[...]
    states = [s_s[j] for j in range(DN_HEADS)]
    for i in range(nck):
        sl = slice(i * c, (i + 1) * c)
        ns = [i * DN_HEADS + j for j in range(DN_HEADS)]
        s_hi = [s.astype(BF16) for s in states]
        s_lo = [(s - h.astype(F32)).astype(BF16) for s, h in zip(states, s_hi)]
        p1 = [jnp.dot(l1[n], h, preferred_element_type=F32) for n, h in zip(ns, s_hi)]
        p2 = [jnp.dot(ke_hi[n], lo, preferred_element_type=F32) for n, lo in zip(ns, s_lo)]
        res = [bc[n] * (vc[n] - (a[:c] + a[c:2 * c] + b)) for n, a, b in zip(ns, p1, p2)]
        p3 = [jnp.dot(w2[n], r.astype(BF16), preferred_element_type=F32) for n, r in zip(ns, res)]
        outs = []
        for j in range(DN_HEADS):
            o = p1[j][2 * c:] + p3[j][DN_DK:]
            states[j] = states[j] * decay[ns[j]] + p3[j][:DN_DK]
            outs.append(o * lax.rsqrt(jnp.mean(o * o, axis=-1, keepdims=True) + EPS) * nw)
        o_ref[0, sl, :] = (jnp.concatenate(outs, axis=-1) * _silu(z_ref[0, sl, :])).astype(o_ref.dtype)
    for j in range(DN_HEADS):
        s_s[j] = states[j]


def deltanet_mixer(rec, conv_w, a_log, dt_bias, out_norm, cols, tc=256):
    bsz, seq, _ = rec.shape
    cq, ck, cv, cz, cab = cols
    hb = tc // 8
    dq = DN_HEADS * DN_DK

    def col(c0):
        return pl.BlockSpec((1, tc, dq), lambda b, t: (b, t, c0 // dq))

    def halo(c0):
        return pl.BlockSpec((1, 8, dq), lambda b, t: (b, jnp.maximum(t * hb - 1, 0), c0 // dq))

    def wcol(c0):
        return pl.BlockSpec((DN_CONV, dq), lambda b, t: (0, c0 // dq))

    def lanes(x):
        return jnp.zeros((1, LANE), F32).at[0, :DN_HEADS].set(x)

    return pl.pallas_call(
        _dn_kernel,
        grid=(bsz, seq // tc),
        in_specs=[col(cq), col(ck), col(cv), halo(cq), halo(ck), halo(cv), col(cz),
                  pl.BlockSpec((1, tc, LANE), lambda b, t: (b, t, cab // LANE)),
                  wcol(0), wcol(dq), wcol(2 * dq), _resident((1, DN_DK)), _resident((1, LANE)), _resident((1, LANE))],
        out_specs=pl.BlockSpec((1, tc, dq), lambda b, t: (b, t, 0)),
        out_shape=jax.ShapeDtypeStruct((bsz, seq, dq), BF16),
        scratch_shapes=[pltpu.VMEM((DN_HEADS, DN_DK, DN_DK), F32)],
        compiler_params=_cparams(("parallel", "arbitrary")),
        name="deltanet",
    )(rec, rec, rec, rec, rec, rec, rec, rec, conv_w, conv_w, conv_w, out_norm.reshape(1, DN_DK),
      lanes(a_log), lanes(dt_bias))


FFN_HALO = 16
FFN_AHEAD = 4


def _ffn_kernel(x_ref, xh_ref, nw_ref, sh_ref, sc_ref, g_ref, wu_ref, cw_ref, wd_ref, out_ref, hb_s, up_s, *, per):
    i = pl.program_id(0)
    tm = x_ref.shape[0]
    nf = wd_ref.shape[0]
    nw, sh, sc = nw_ref[...], sh_ref[0], sc_ref[0]
    x = x_ref[...]
    live = (i % per > 0).astype(F32)
    hb_s[0:FFN_HALO, :] = (_modulate(xh_ref[...], nw, sc, sh) * live).astype(BF16)
    hb_s[FFN_HALO:, :] = _modulate(x, nw, sc, sh).astype(BF16)
    hb = hb_s[...]

    def conv(slot, half, w):
        y = w[FFN_CONV - 1:FFN_CONV, :] * up_s[slot, half, FFN_HALO:, :]
        for s in range(1, FFN_CONV):
            y = y + w[FFN_CONV - 1 - s:FFN_CONV - s, :] * up_s[slot, half, FFN_HALO - s:FFN_HALO - s + tm, :]
        return y

    nslot, cw = up_s.shape[0], up_s.shape[-1]
    cols = lambda half, f: slice(half * nf * cw + f * cw, half * nf * cw + (f + 1) * cw)

    def up(f):
        for half in range(2):
            up_s[f % nslot, half] = jnp.dot(hb, wu_ref[:, cols(half, f)], preferred_element_type=F32)

    acc = jnp.zeros((tm, out_ref.shape[-1]), F32)
    for f in range(min(FFN_AHEAD, nf)):
        up(f)
    for f in range(nf):
        if f + FFN_AHEAD < nf:
            up(f + FFN_AHEAD)
        act = _silu(conv(f % nslot, 0, cw_ref[:, cols(0, f)])) * conv(f % nslot, 1, cw_ref[:, cols(1, f)])
        acc = acc + jnp.dot(act.astype(BF16), wd_ref[f], preferred_element_type=F32)
    out_ref[...] = x + g_ref[0] * acc


def conv_ffn(x2, seq, nw, shift, scale, gate, w_up, conv_w, w_down, tm=512, cw=256):
    n, d = x2.shape
    dff = w_down.shape[0]
    nf = dff // cw
    per = seq // tm
    wu = w_up.astype(BF16)
    cwt = conv_w
    wd = w_down.astype(BF16).reshape(nf, cw, d)
    row = lambda i: (i, 0)
    bat = lambda i: (i // per, 0, 0)
    hpt = tm // FFN_HALO
    return pl.pallas_call(
        functools.partial(_ffn_kernel, per=per),
        grid=(n // tm,),
        in_specs=[pl.BlockSpec((tm, d), row),
                  pl.BlockSpec((FFN_HALO, d), lambda i: (jnp.maximum(i * hpt - 1, 0), 0)),
                  _resident((1, d)), pl.BlockSpec((1, 1, d), bat), pl.BlockSpec((1, 1, d), bat),
                  pl.BlockSpec((1, 1, d), bat),
                  _resident(wu.shape), _resident(cwt.shape), _resident(wd.shape)],
        out_specs=pl.BlockSpec((tm, d), row),
        out_shape=jax.ShapeDtypeStruct((n, d), F32),
        scratch_shapes=[pltpu.VMEM((FFN_HALO + tm, d), BF16),
                        pltpu.VMEM((FFN_AHEAD + 1, 2, FFN_HALO + tm, cw), F32)],
        compiler_params=_cparams(("parallel",)),
        name="ffn",
    )(x2, x2, nw.reshape(1, d), shift, scale, gate, wu, cwt, wd)


def _alibi_slopes(n):
    return jnp.asarray(2.0 ** (-8.0 * np.arange(1, n + 1) / n), dtype=F32)


def _pad_cols(w, mult):
    pad = (-w.shape[1]) % mult
    return jnp.pad(w, ((0, 0), (0, pad)))


def _rec_weight_layout(w):
    dq = DN_HEADS * DN_DK
    nu = S5_WIDTH
    w2 = _pad_cols(jnp.concatenate([w[:, nu:nu + 4 * dq], w[:, :nu], w[:, nu + 4 * dq:]], axis=1), LANE)
    cols = (0, dq, 2 * dq, 3 * dq, 4 * dq + nu)
    return w2, cols, 4 * dq


def kernel(x, c, ada_w, ada_b, norm_mix, norm_ffn, attn_w_in, attn_q_norm_a, attn_k_norm_a, attn_q_norm_b,
           attn_k_norm_b, attn_sinks, attn_w_out, rec_w_in, s5_lambda_re, s5_lambda_im, s5_log_dt, s5_b_re,
           s5_b_im, s5_c_re, s5_c_im, s5_d, s5_glu_w, s5_glu_b, dn_conv, dn_a_log, dn_dt_bias, dn_out_norm,
           rec_w_out, ffn_w_up, ffn_conv, ffn_w_down):
    bsz, seq, d = x.shape
    depth = ada_w.shape[0]
    mod = adaln(c, ada_w, ada_b)
    x2 = x.reshape(bsz * seq, d)
    slopes = _alibi_slopes(N_ATTN_HEADS)
    for layer in range(depth):
        sh1, sc1, g1, sh2, sc2, g2 = [mod[layer, :, j * d:(j + 1) * d].reshape(bsz, 1, d) for j in range(6)]
        i = layer // 2
        if layer % 2 == 0:
            qkv = inproj(x2, seq, norm_mix[layer], sh1, sc1, attn_w_in[i].astype(BF16))
            qkv = qkv.reshape(bsz, seq, -1)
            oa, ob = attention(qkv, slopes, attn_sinks[i], attn_q_norm_a[i], attn_k_norm_a[i],
                               attn_q_norm_b[i], attn_k_norm_b[i])
            x2 = mix_out(x2, seq, g1, oa.reshape(bsz * seq, -1), ob.reshape(bsz * seq, -1),
                         attn_w_out[i].astype(BF16), "attn_out")
        else:
            w_rec, rec_cols, ucol = _rec_weight_layout(rec_w_in[i])
            rec = inproj(x2, seq, norm_mix[layer], sh1, sc1, w_rec.astype(BF16))
            rec = rec.reshape(bsz, seq, -1)
            yc = s5_mixer(rec, s5_lambda_re[i], s5_lambda_im[i], s5_log_dt[i], s5_b_re[i], s5_b_im[i],
                          s5_c_re[i], s5_c_im[i], s5_d[i], s5_glu_w[i], s5_glu_b[i], ucol=ucol)
            yd = deltanet_mixer(rec, dn_conv[i], dn_a_log[i], dn_dt_bias[i], dn_out_norm[i], rec_cols)
            x2 = mix_out(x2, seq, g1, yc.reshape(bsz * seq, -1), yd.reshape(bsz * seq, -1),
                         rec_w_out[i].astype(BF16), "rec_out")
        x2 = conv_ffn(x2, seq, norm_ffn[layer], sh2, sc2, g2, ffn_w_up[layer], ffn_conv[layer], ffn_w_down[layer])
    return x2.reshape(bsz, seq, d)
```

```python
import functools
import math

import numpy as np
import jax
import jax.numpy as jnp
from jax import lax
from jax.experimental import pallas as pl
from jax.experimental.pallas import tpu as pltpu

F32 = jnp.float32
BF16 = jnp.bfloat16
HIGHEST = lax.Precision.HIGHEST
EPS = 1e-6
NEG = -1e30

HEAD_DIM = 64
BLOCK = 128
A_Q_HEADS = 8
A_KV_HEADS = 2
A_WINDOW = 128
B_HEADS = 8
B_BRANCHES = ((128, 1), (512, 4), (2048, 16))
N_ATTN_HEADS = A_Q_HEADS + B_HEADS
S5_GROUP = 16
S5_GROUPS = 16
S5_WIDTH = 256
S5_STATE = 64
DN_HEADS = 6
DN_DK = 128
DN_CONV = 4
DN_CHUNK = 64
FFN_CONV = 3

V7X_VMEM_LIMIT = 60 * 1024 * 1024
LANE = 128


def _cparams(sem, vmem=V7X_VMEM_LIMIT):
    return pltpu.CompilerParams(dimension_semantics=sem, vmem_limit_bytes=vmem)


def _resident(shape):
    n = len(shape)
    return pl.BlockSpec(shape, lambda *_: (0,) * n, pipeline_mode=pl.Buffered(1))


def _sigmoid(x):
    return 1.0 / (1.0 + jnp.exp(-x))


def _silu(x):
    h = 0.5 * x
    return h + h * jnp.tanh(h)


def _modulate(x, nw, scale, shift):
    ms = jnp.mean(x * x, axis=-1, keepdims=True)
    y = x * lax.rsqrt(ms + EPS) * nw
    return y * (1.0 + scale) + shift


def _adaln_kernel(c_ref, w_ref, b_ref, o_ref):
    c = c_ref[...]
    o_ref[0] = jnp.dot(_silu(c), w_ref[0], precision=HIGHEST, preferred_element_type=F32) + b_ref[0]


def adaln(c, ada_w, ada_b, tn=1536):
    depth, d, n = ada_w.shape
    bsz = c.shape[0]
    cp = jnp.zeros((8, d), F32).at[:bsz].set(c)
    out = pl.pallas_call(
        _adaln_kernel,
        grid=(depth, n // tn),
        in_specs=[pl.BlockSpec((8, d), lambda l, j: (0, 0)),
                  pl.BlockSpec((1, d, tn), lambda l, j: (l, 0, j)),
                  pl.BlockSpec((1, 1, tn), lambda l, j: (l, 0, j))],
        out_specs=pl.BlockSpec((1, 8, tn), lambda l, j: (l, 0, j)),
        out_shape=jax.ShapeDtypeStruct((depth, 8, n), F32),
        compiler_params=_cparams(("arbitrary", "arbitrary")),
        name="adaln",
    )(cp, ada_w, ada_b.reshape(depth, 1, n))
    return out[:, :bsz]


def _inproj_kernel(x_ref, nw_ref, sh_ref, sc_ref, w_ref, o_ref):
    h = _modulate(x_ref[...], nw_ref[...], sc_ref[0], sh_ref[0]).astype(BF16)
    o_ref[...] = jnp.dot(h, w_ref[...], preferred_element_type=F32)


def inproj(x2, seq, nw, shift, scale, w, tm=512):
    n, d = x2.shape
    nout = w.shape[1]
    per = seq // tm
    return pl.pallas_call(
        _inproj_kernel,
        grid=(n // tm,),
        in_specs=[pl.BlockSpec((tm, d), lambda i: (i, 0)),
                  _resident((1, d)),
                  pl.BlockSpec((1, 1, d), lambda i: (i // per, 0, 0)),
                  pl.BlockSpec((1, 1, d), lambda i: (i // per, 0, 0)),
                  _resident((d, nout))],
        out_specs=pl.BlockSpec((tm, nout), lambda i: (i, 0)),
        out_shape=jax.ShapeDtypeStruct((n, nout), F32),
        compiler_params=_cparams(("parallel",)),
        name="inproj",
    )(x2, nw.reshape(1, d), shift, scale, w)


CONV_HALO = 16


def _inproj_conv_kernel(x_ref, xh_ref, nw_ref, sh_ref, sc_ref, w_ref, cw_ref, o_ref, hb_s, up_s, *, per, chunk):
    i = pl.program_id(0)
    tm = x_ref.shape[0]
    nout = o_ref.shape[-1]
    ntap, nconv = cw_ref.shape
    nw, sh, sc = nw_ref[...], sh_ref[0], sc_ref[0]
    live = (i % per > 0).astype(F32)
    hb_s[0:CONV_HALO, :] = (_modulate(xh_ref[...], nw, sc, sh) * live).astype(BF16)
    hb_s[CONV_HALO:, :] = _modulate(x_ref[...], nw, sc, sh).astype(BF16)
    hb = hb_s[...]
    bounds = [(c0, min(c0 + chunk, nout)) for c0 in range(0, nout, chunk)]

    def project(n):
        c0, c1 = bounds[n]
        up_s[n % 2, :, 0:c1 - c0] = jnp.dot(hb, w_ref[:, c0:c1], preferred_element_type=F32)

    def finish(n):
        c0, c1 = bounds[n]
        wid = c1 - c0
        if c0 < nconv:
            w = cw_ref[:, c0:c1]
            y = w[ntap - 1:ntap, :] * up_s[n % 2, CONV_HALO:, 0:wid]
            for s in range(1, ntap):
                y = y + w[ntap - 1 - s:ntap - s, :] * up_s[n % 2, CONV_HALO - s:CONV_HALO - s + tm, 0:wid]
            o_ref[:, c0:c1] = _silu(y)
        else:
            o_ref[:, c0:c1] = up_s[n % 2, CONV_HALO:, 0:wid]

    project(0)
    for n in range(len(bounds)):
        if n + 1 < len(bounds):
            project(n + 1)
        finish(n)


def inproj_conv(x2, seq, nw, shift, scale, w, conv_w, tm=512, chunk=768):
    n, d = x2.shape
    nout = w.shape[1]
    per = seq // tm
    hpt = tm // CONV_HALO
    assert conv_w.shape[1] % chunk == 0
    return pl.pallas_call(
        functools.partial(_inproj_conv_kernel, per=per, chunk=chunk),
        grid=(n // tm,),
        in_specs=[pl.BlockSpec((tm, d), lambda i: (i, 0)),
                  pl.BlockSpec((CONV_HALO, d), lambda i: (jnp.maximum(i * hpt - 1, 0), 0)),
                  _resident((1, d)),
                  pl.BlockSpec((1, 1, d), lambda i: (i // per, 0, 0)),
                  pl.BlockSpec((1, 1, d), lambda i: (i // per, 0, 0)),
                  _resident((d, nout)), _resident(conv_w.shape)],
        out_specs=pl.BlockSpec((tm, nout), lambda i: (i, 0)),
        out_shape=jax.ShapeDtypeStruct((n, nout), F32),
        scratch_shapes=[pltpu.VMEM((CONV_HALO + tm, d), BF16), pltpu.VMEM((2, CONV_HALO + tm, chunk), F32)],
        compiler_params=_cparams(("parallel",)),
        name="inproj_conv",
    )(x2, x2, nw.reshape(1, d), shift, scale, w, conv_w)


ATT_SPAN = 2048
ATT_GROUP = 16
ATT_PREP = 512


def _static_or_multiple(x, k):
    return x if isinstance(x, int) else pl.multiple_of(x, k)


def _attn_kernel(slopes_ref, sinks_ref, q_ref, k_ref, v_ref, qw_ref, kw_ref, o_ref, *scratch,
                 branches, head0, with_sinks, dup_kv):
    p = pl.program_id(1)
    t = pl.program_id(2)
    span = q_ref.shape[1]
    nbr = len(branches)
    assert branches[0][0] == 1
    need32 = nbr > 1
    it = iter(scratch)
    qn_s, kn_s = (next(it), next(it)) if need32 else (None, None)
    per = [(next(it), next(it), next(it), next(it)) for _ in branches]
    acc_s, m_s, l_s = (next(it), next(it), next(it)) if nbr > 1 else (None, None, None)

    lane = lax.broadcasted_iota(jnp.int32, (1, LANE), 1)
    low = lane < HEAD_DIM
    li = lax.broadcasted_iota(jnp.int32, (LANE, LANE), 0) // HEAD_DIM
    lj = lax.broadcasted_iota(jnp.int32, (LANE, LANE), 1) // HEAD_DIM
    same_head = (li == lj).astype(BF16)

    for (d, _), (_, ks, vs, _) in zip(branches, per):
        n_d = span // d

        @pl.when(t == 0)
        def _():
            for r in range(d):
                ks[r, 0:BLOCK, :] = jnp.zeros((BLOCK, LANE), BF16)
                for j in range(2):
                    vs[j, r, 0:BLOCK, :] = jnp.zeros((BLOCK, LANE), BF16)

        @pl.when(t > 0)
        def _():
            for r in range(d):
                ks[r, 0:BLOCK, :] = ks[r, n_d:n_d + BLOCK, :]
                for j in range(2):
                    vs[j, r, 0:BLOCK, :] = vs[j, r, n_d:n_d + BLOCK, :]

    @pl.when(t == 0)
    def _():
        qi = lax.broadcasted_iota(jnp.int32, (BLOCK, 2 * BLOCK), 0)
        si = lax.broadcasted_iota(jnp.int32, (BLOCK, 2 * BLOCK), 1)
        dist = BLOCK + qi - si
        for (d, max_dist), (_, _, _, bias) in zip(branches, per):
            valid = (dist >= 0) & (dist <= max_dist)
            valid0 = valid & (si >= BLOCK)
            distf = (d * dist).astype(F32)
            for j in range(2):
                b = -slopes_ref[head0 + 2 * p + j] * distf
                bias[0, j] = jnp.where(valid0, b, NEG)
                bias[1, j] = jnp.where(valid, b, NEG)

    def head_norm(x, w):
        x2 = x * x
        hi = x2.astype(BF16)
        lo = (x2 - hi.astype(F32)).astype(BF16)
        ss = jnp.dot(hi, same_head, preferred_element_type=F32) + jnp.dot(lo, same_head, preferred_element_type=F32)
        return x * lax.rsqrt(ss * (1.0 / HEAD_DIM) + EPS) * w

    def both_halves(x):
        swapped = pltpu.roll(x, HEAD_DIM, 1)
        return jnp.where(lane // HEAD_DIM == p // 2, x, swapped)

    def put_q(qs, rows, x):
        qs[0, rows, :] = jnp.where(low, x, 0.0).astype(BF16)
        qs[1, rows, :] = jnp.where(low, 0.0, x).astype(BF16)

    def put_v(vs, r, rows, x):
        vs[0, r, rows, :] = jnp.where(low, x, 1.0).astype(BF16)
        vs[1, r, rows, :] = jnp.where(low, 1.0, x).astype(BF16)

    qw, kw = qw_ref[...], kw_ref[...]
    qs1, ks1, vs1, _ = per[0]

    def prep(c, _):
        r0 = pl.multiple_of(c * ATT_PREP, ATT_PREP)
        rows = pl.ds(r0, ATT_PREP)
        krows = pl.ds(pl.multiple_of(BLOCK + r0, BLOCK), ATT_PREP)
        qn = head_norm(q_ref[0, rows, :], qw) * (HEAD_DIM ** -0.5)
        xk, xv = k_ref[0, rows, :], v_ref[0, rows, :]
        if dup_kv:
            xk, xv = both_halves(xk), both_halves(xv)
        kn = head_norm(xk, kw)
        if need32:
            qn_s[rows, :] = qn
            kn_s[rows, :] = kn
        put_q(qs1, rows, qn)
        ks1[0, krows, :] = kn.astype(BF16)
        put_v(vs1, 0, krows, xv)
        return 0

    lax.fori_loop(0, span // ATT_PREP, prep, 0)
    for (d, _), (qs, ks, vs, _) in zip(branches[1:], per[1:]):
        n_d = span // d
        for r in range(d):
            sub = pl.ds(r, n_d, stride=d)
            put_q(qs, slice(r * n_d, (r + 1) * n_d), qn_s[sub, :])
            ks[r, BLOCK:BLOCK + n_d, :] = kn_s[sub, :].astype(BF16)
            put_v(vs, r, slice(BLOCK, BLOCK + n_d), v_ref[0, sub, :])

    def process(bi, combos):
        d, _ = branches[bi]
        qs, ks, vs, bias = per[bi]
        n_d = span // d
        qv, kv, vv, bs, out_rows = [], [], [], [], []
        for r, n in combos:
            qrow = _static_or_multiple(r * n_d + n * BLOCK, BLOCK)
            krow = _static_or_multiple(n * BLOCK, BLOCK)
            bidx = jnp.where(jnp.logical_and(t == 0, n == 0), 0, 1)
            for j in range(2):
                qv.append(qs[j, pl.ds(qrow, BLOCK), :])
                kv.append(ks[r, pl.ds(krow, 2 * BLOCK), :])
                vv.append(vs[j, r, pl.ds(krow, 2 * BLOCK), :])
                bs.append(bias[bidx, j])
            out_rows.append(pl.ds(krow, BLOCK) if d == 1 else pl.ds(r + d * BLOCK * n, BLOCK, stride=d))
        s = [lax.dot_general(q, k, (((1,), (1,)), ((), ())), preferred_element_type=F32) + b
             for q, k, b in zip(qv, kv, bs)]
        m = [jnp.max(x, axis=-1, keepdims=True) for x in s]
        pr = [jnp.exp(x - mm).astype(BF16) for x, mm in zip(s, m)]
        acc = [jnp.dot(x, v, preferred_element_type=F32) for x, v in zip(pr, vv)]
        for ci, rows in enumerate(out_rows):
            a0, a1, m0, m1 = acc[2 * ci], acc[2 * ci + 1], m[2 * ci], m[2 * ci + 1]
            num = jnp.where(low, a0, a1)
            den = pltpu.roll(jnp.where(low, a1, a0), HEAD_DIM, 1)
            mx = jnp.where(low, m0, m1)
            if nbr == 1:
                if with_sinks:
                    sk = jnp.where(low, sinks_ref[2 * p], sinks_ref[2 * p + 1])
                    den = den + jnp.exp(sk - mx)
                o_ref[0, rows, :] = (num / den).astype(o_ref.dtype)
            elif bi == 0:
                acc_s[rows, :] = num
                l_s[rows, :] = den
                m_s[rows, :] = mx
            else:
                m_old = m_s[rows, :]
                m_new = jnp.maximum(m_old, mx)
                w_old, w_new = jnp.exp(m_old - m_new), jnp.exp(mx - m_new)
                num = w_old * acc_s[rows, :] + w_new * num
                den = w_old * l_s[rows, :] + w_new * den
                if bi == nbr - 1:
                    acc_s[rows, :] = num / den
                else:
                    acc_s[rows, :] = num
                    l_s[rows, :] = den
                    m_s[rows, :] = m_new

    for bi, (d, _) in enumerate(branches):
        nblk = span // d // BLOCK

        def group(g, _, bi=bi, nblk=nblk):
            if nblk >= ATT_GROUP:
                base = g * ATT_GROUP
                r = base // nblk
                combos = [(r, base % nblk + i) for i in range(ATT_GROUP)]
            else:
                per_grp = ATT_GROUP // nblk
                combos = [(g * per_grp + i // nblk, i % nblk) for i in range(ATT_GROUP)]
            process(bi, combos)
            return 0

        lax.fori_loop(0, span // BLOCK // ATT_GROUP, group, 0)
    if nbr > 1:
        o_ref[0] = acc_s[...].astype(o_ref.dtype)


def _attention_call(qkv, slopes, sinks, qw, kw, *, branches, qblk, kblk, vblk, shared_kv, head0, with_sinks, name):
    bsz, seq, _ = qkv.shape
    span = min(seq, ATT_SPAN)
    npair = 4
    kmap = (lambda b, p, t: (b, t, kblk)) if shared_kv else (lambda b, p, t: (b, t, kblk + p))
    vmap = (lambda b, p, t: (b, t, vblk)) if shared_kv else (lambda b, p, t: (b, t, vblk + p))
    smem = pl.BlockSpec(memory_space=pltpu.SMEM)
    scratch = []
    if len(branches) > 1:
        scratch += [pltpu.VMEM((span, LANE), F32), pltpu.VMEM((span, LANE), F32)]
    for d, _ in branches:
        n_d = span // d
        scratch += [pltpu.VMEM((2, span, LANE), BF16), pltpu.VMEM((d, BLOCK + n_d, LANE), BF16),
                    pltpu.VMEM((2, d, BLOCK + n_d, LANE), BF16), pltpu.VMEM((2, 2, BLOCK, 2 * BLOCK), F32)]
    if len(branches) > 1:
        scratch += [pltpu.VMEM((span, LANE), F32)] * 3
    tile2 = lambda w: jnp.tile(w.reshape(1, HEAD_DIM), (1, 2))
    return pl.pallas_call(
        functools.partial(_attn_kernel, branches=branches, head0=head0, with_sinks=with_sinks, dup_kv=shared_kv),
        grid=(bsz, npair, seq // span),
        in_specs=[smem, smem,
                  pl.BlockSpec((1, span, LANE), lambda b, p, t: (b, t, qblk + p)),
                  pl.BlockSpec((1, span, LANE), kmap),
                  pl.BlockSpec((1, span, LANE), vmap),
                  _resident((1, LANE)), _resident((1, LANE))],
        out_specs=pl.BlockSpec((1, span, LANE), lambda b, p, t: (b, t, p)),
        out_shape=jax.ShapeDtypeStruct((bsz, seq, npair * LANE), BF16),
        scratch_shapes=scratch,
        compiler_params=_cparams(("parallel", "parallel", "arbitrary")),
        name=name,
    )(slopes, sinks, qkv, qkv, qkv, tile2(qw), tile2(kw))


def attention(qkv, slopes, sinks, qw_a, kw_a, qw_b, kw_b):
    oa = _attention_call(qkv, slopes, sinks, qw_a, kw_a, branches=((1, A_WINDOW - 1),), qblk=0, kblk=4, vblk=5,
                         shared_kv=True, head0=0, with_sinks=True, name="attn_a")
    ob = _attention_call(qkv, slopes, sinks, qw_b, kw_b, branches=tuple((d, w // d) for w, d in B_BRANCHES),
                         qblk=6, kblk=10, vblk=14, shared_kv=False, head0=A_Q_HEADS, with_sinks=False, name="attn_b")
    return oa, ob


def _mix_out_kernel(x_ref, g_ref, ya_ref, yb_ref, w_ref, out_ref):
    na = ya_ref.shape[-1]
    y = jnp.dot(ya_ref[...], w_ref[:na, :], preferred_element_type=F32)
    y = y + jnp.dot(yb_ref[...], w_ref[na:, :], preferred_element_type=F32)
    out_ref[...] = x_ref[...] + g_ref[0] * y


def mix_out(x2, seq, gate, ya, yb, w, name, tm=512):
    n, d = x2.shape
    per = seq // tm
    row = lambda i: (i, 0)
    return pl.pallas_call(
        _mix_out_kernel,
        grid=(n // tm,),
        in_specs=[pl.BlockSpec((tm, d), row), pl.BlockSpec((1, 1, d), lambda i: (i // per, 0, 0)),
                  pl.BlockSpec((tm, ya.shape[-1]), row), pl.BlockSpec((tm, yb.shape[-1]), row), _resident(w.shape)],
        out_specs=pl.BlockSpec((tm, d), row),
        out_shape=jax.ShapeDtypeStruct((n, d), F32),
        compiler_params=_cparams(("parallel",)),
        name=name,
    )(x2, gate, ya, yb, w)


def _gelu_tanh(x):
    return 0.5 * x * (1.0 + jnp.tanh(math.sqrt(2.0 / math.pi) * (x + 0.044715 * (x * x * x))))


def _s5_kernel(u_ref, e_ref, pr_ref, pi_ref, cr_ref, ci_ref, d_ref, gw_ref, gb_ref, o_ref, xr_s, xi_s, car_s):
    t = pl.program_id(1)
    tc = u_ref.shape[1]
    ns = pr_ref.shape[1]

    @pl.when(t == 0)
    def _():
        car_s[...] = jnp.zeros_like(car_s)

    u = u_ref[0]
    e = jnp.dot(u.astype(BF16), e_ref[...], preferred_element_type=F32)
    xr, xi = e[:, :ns], e[:, ns:]
    xr = xr.reshape(tc // 8, 8, ns)
    xi = xi.reshape(tc // 8, 8, ns)
    row = lax.broadcasted_iota(jnp.int32, (tc // 8, 8, ns), 1)
    for s in (1, 2, 4):
        ar, ai = pr_ref[s - 1:s, :], pi_ref[s - 1:s, :]
        keep = row >= s
        sr = jnp.where(keep, pltpu.roll(xr, s, 1), 0.0)
        si = jnp.where(keep, pltpu.roll(xi, s, 1), 0.0)
        xr, xi = xr + (ar * sr - ai * si), xi + (ar * si + ai * sr)
    xr_s[...] = xr.reshape(tc, ns)
    xi_s[...] = xi.reshape(tc, ns)
    pr, pi = pr_ref[...], pi_ref[...]

    def grp(k, carry):
        cr, ci = carry
        r0 = pl.multiple_of(k * 8, 8)
        nr = xr_s[pl.ds(r0, 8), :] + (pr * cr - pi * ci)
        ni = xi_s[pl.ds(r0, 8), :] + (pr * ci + pi * cr)
        xr_s[pl.ds(r0, 8), :] = nr
        xi_s[pl.ds(r0, 8), :] = ni
        return nr[7:8, :], ni[7:8, :]

    cr, ci = lax.fori_loop(0, tc // 8, grp, (car_s[0:1, :], car_s[1:2, :]))
    car_s[0:1, :] = cr
    car_s[1:2, :] = ci
    y = (jnp.dot(xr_s[...].astype(BF16), cr_ref[...], preferred_element_type=F32)
         - jnp.dot(xi_s[...].astype(BF16), ci_ref[...], preferred_element_type=F32)
         + d_ref[...] * u)
    g = _gelu_tanh(y)
    z = jnp.dot(g.astype(BF16), gw_ref[...], preferred_element_type=F32) + gb_ref[...]
    o_ref[0] = (g * _sigmoid(z)).astype(o_ref.dtype)


def s5_mixer(rec, lam_re, lam_im, log_dt, b_re, b_im, c_re, c_im, d_skip, glu_w, glu_b, ucol=0, tc=512):
    bsz, seq, _ = rec.shape
    ublk = ucol // S5_WIDTH
    ns = S5_GROUPS * S5_STATE
    dt = jnp.exp(log_dt)[:, None]
    lr, li = lam_re, lam_im
    mag, ang = jnp.exp(lr * dt), li * dt
    ab_re, ab_im = mag * jnp.cos(ang), mag * jnp.sin(ang)
    nr, ni = ab_re - 1.0, ab_im
    den = lr * lr + li * li
    f_re = (nr * lr + ni * li) / den
    f_im = (ni * lr - nr * li) / den
    kk = jnp.arange(1, 9, dtype=F32)[:, None, None]
    pmag = jnp.exp(kk * (lr * dt)[None])
    p_re = (pmag * jnp.cos(kk * ang[None])).reshape(8, ns)
    p_im = (pmag * jnp.sin(kk * ang[None])).reshape(8, ns)
    e_re = f_re[..., None] * b_re - f_im[..., None] * b_im
    e_im = f_re[..., None] * b_im + f_im[..., None] * b_re
    eye = jnp.eye(S5_GROUPS, dtype=F32)

    def bd_in(m):
        return jnp.einsum("gpi,gh->gihp", m, eye).reshape(S5_WIDTH, ns)

    def bd_out(m):
        return jnp.einsum("gip,gh->gphi", m, eye).reshape(ns, S5_WIDTH)

    e_mat = jnp.concatenate([bd_in(e_re), bd_in(e_im)], axis=1).astype(BF16)
    return pl.pallas_call(
        _s5_kernel,
        grid=(bsz, seq // tc),
        in_specs=[pl.BlockSpec((1, tc, S5_WIDTH), lambda b, t: (b, t, ublk)),
                  _resident((S5_WIDTH, 2 * ns)), _resident((8, ns)), _resident((8, ns)),
                  _resident((ns, S5_WIDTH)), _resident((ns, S5_WIDTH)),
                  _resident((1, S5_WIDTH)), _resident((S5_WIDTH, S5_WIDTH)), _resident((1, S5_WIDTH))],
        out_specs=pl.BlockSpec((1, tc, S5_WIDTH), lambda b, t: (b, t, 0)),
        out_shape=jax.ShapeDtypeStruct((bsz, seq, S5_WIDTH), BF16),
        scratch_shapes=[pltpu.VMEM((tc, ns), F32), pltpu.VMEM((tc, ns), F32), pltpu.VMEM((8, ns), F32)],
        compiler_params=_cparams(("parallel", "arbitrary")),
        name="s5",
    )(rec, e_mat, p_re, p_im, bd_out(c_re).astype(BF16), bd_out(c_im).astype(BF16),
      d_skip.reshape(1, S5_WIDTH), glu_w.astype(BF16), glu_b.reshape(1, S5_WIDTH))


def _dot_bf(a, b):
    return jnp.dot(a.astype(BF16), b.astype(BF16), preferred_element_type=F32)


def _dot_nt_bf(a, b):
    return lax.dot_general(a.astype(BF16), b.astype(BF16), (((1,), (1,)), ((), ())), preferred_element_type=F32)


def _dot_tn_bf(a, b):
    return lax.dot_general(a.astype(BF16), b.astype(BF16), (((0,), (0,)), ((), ())), preferred_element_type=F32)


def _split_bf(x):
    hi = x.astype(BF16)
    return hi, (x - hi.astype(F32)).astype(BF16)


def _dot_x3(a, b):
    ah, al = _split_bf(a)
    bh, bl = _split_bf(b)
    return (jnp.dot(ah, bh, preferred_element_type=F32) + jnp.dot(ah, bl, preferred_element_type=F32)
            + jnp.dot(al, bh, preferred_element_type=F32))


def _dot_01(m01, x):
    hi = x.astype(BF16)
    r = x - hi.astype(F32)
    mid = r.astype(BF16)
    lo = (r - mid.astype(F32)).astype(BF16)
    return (jnp.dot(m01, hi, preferred_element_type=F32) + jnp.dot(m01, mid, preferred_element_type=F32)
            + jnp.dot(m01, lo, preferred_element_type=F32))


def _dn_kernel(q_ref, k_ref, v_ref, z_ref, ab_ref, nw_ref, alog_ref, dtb_ref, o_ref, s_s):
    t = pl.program_id(1)
    tc = q_ref.shape[1]
    c = DN_CHUNK

    @pl.when(t == 0)
    def _():
        s_s[...] = jnp.zeros_like(s_s)

    def head_l2(x2, scale):
        out = []
        for j in range(DN_HEADS):
            x = x2[:, j * DN_DK:(j + 1) * DN_DK]
            out.append(x * lax.rsqrt(jnp.sum(x * x, axis=-1, keepdims=True) + EPS) * scale)
        return out

    qn = head_l2(q_ref[0], DN_DK ** -0.5)
    kn = head_l2(k_ref[0], 1.0)
    v2 = v_ref[0]
    ab = ab_ref[0]
    bfull = _sigmoid(ab)
    xa = ab + dtb_ref[...]
    gfull = -jnp.exp(alog_ref[...]) * (jnp.maximum(xa, 0.0) + jnp.log(1.0 + jnp.exp(-jnp.abs(xa))))

    ri = lax.broadcasted_iota(jnp.int32, (c, c), 0)
    ci = lax.broadcasted_iota(jnp.int32, (c, c), 1)
    causal = ri >= ci
    strict = ri > ci
    tril = causal.astype(BF16)
    triu = (ri <= ci).astype(F32)
    ones = jnp.ones((c, c), BF16)

    nck = tc // c
    probs = [(i, j) for i in range(nck) for j in range(DN_HEADS)]
    rows = lambda x, i: x[i * c:(i + 1) * c]
    qc = [rows(qn[j], i) for i, j in probs]
    kc = [rows(kn[j], i) for i, j in probs]
    vc = [rows(v2[:, j * DN_DK:(j + 1) * DN_DK], i) for i, j in probs]
    bc = [rows(bfull[:, DN_HEADS + j:DN_HEADS + j + 1], i) for i, j in probs]
    gb = [jnp.broadcast_to(rows(gfull[:, j:j + 1], i), (c, c)) for i, j in probs]
    cs_l = jnp.concatenate([tril, -ones], axis=1)
    zero = jnp.zeros((c, c), F32)
    cs = [_dot_01(cs_l, jnp.concatenate([jnp.concatenate([x, x], axis=1),
                                         jnp.concatenate([x * triu, zero], axis=1)], axis=0)) for x in gb]
    gdiff = [x[:, :c] for x in cs]
    gi = [x[:, c:c + 1] for x in cs]
    glast = [x[c - 1:c, c:c + 1] for x in cs]
    qkk = [_dot_nt_bf(jnp.concatenate([x, y], axis=0), y) for x, y in zip(qc, kc)]
    gamma = [jnp.where(causal, jnp.exp(jnp.where(causal, x, 0.0)), 0.0) for x in gdiff]
    nm = [jnp.where(strict, b * x[c:] * gm, 0.0) for b, x, gm in zip(bc, qkk, gamma)]
    qk = [x[:c] for x in qkk]
    eye = (ri == ci).astype(F32)
    inv = [eye - x for x in nm]
    pw = [_dot_x3(x, x) for x in nm]
    span = 2
    while span < c:
        inv = [x + _dot_x3(x, y) for x, y in zip(inv, pw)]
        span *= 2
        if span < c:
            pw = [_dot_x3(y, y) for y in pw]
    eg = [jnp.exp(x) for x in gi]
    qk = [x * gm for x, gm in zip(qk, gamma)]
    kd = [k * jnp.exp(gl - g) for k, gl, g in zip(kc, glast, gi)]
    w2 = [jnp.concatenate([_dot_tn_bf(x, t), _dot_bf(y, t)], axis=0).astype(BF16) for x, y, t in zip(kd, qk, inv)]
    ke = [k * e for k, e in zip(kc, eg)]
    ke_hi = [x.astype(BF16) for x in ke]
    ke_lo = [(x - h.astype(F32)).astype(BF16) for x, h in zip(ke, ke_hi)]
    l1 = [jnp.concatenate([h, lo, (q * e).astype(BF16)], axis=0) for h, lo, q, e in zip(ke_hi, ke_lo, qc, eg)]
    decay = [jnp.exp(x) for x in glast]

    nw = nw_ref[...]
    states = [s_s[j] for j in range(DN_HEADS)]
    for i in range(nck):
        sl = slice(i * c, (i + 1) * c)
        ns = [i * DN_HEADS + j for j in range(DN_HEADS)]
        s_hi = [s.astype(BF16) for s in states]
        s_lo = [(s - h.astype(F32)).astype(BF16) for s, h in zip(states, s_hi)]
        p1 = [jnp.dot(l1[n], h, preferred_element_type=F32) for n, h in zip(ns, s_hi)]
        p2 = [jnp.dot(ke_hi[n], lo, preferred_element_type=F32) for n, lo in zip(ns, s_lo)]
        res = [bc[n] * (vc[n] - (a[:c] + a[c:2 * c] + b)) for n, a, b in zip(ns, p1, p2)]
        p3 = [jnp.dot(w2[n], r.astype(BF16), preferred_element_type=F32) for n, r in zip(ns, res)]
        outs = []
        for j in range(DN_HEADS):
            o = p1[j][2 * c:] + p3[j][DN_DK:]
            states[j] = states[j] * decay[ns[j]] + p3[j][:DN_DK]
            outs.append(o * lax.rsqrt(jnp.mean(o * o, axis=-1, keepdims=True) + EPS) * nw)
        o_ref[0, sl, :] = (jnp.concatenate(outs, axis=-1) * _silu(z_ref[0, sl, :])).astype(o_ref.dtype)
    for j in range(DN_HEADS):
        s_s[j] = states[j]


def deltanet_mixer(rec, a_log, dt_bias, out_norm, cols, tc=256):
    bsz, seq, _ = rec.shape
    cq, ck, cv, cz, cab = cols
    dq = DN_HEADS * DN_DK

    def col(c0):
        return pl.BlockSpec((1, tc, dq), lambda b, t: (b, t, c0 // dq))

    def lanes(x):
        return jnp.zeros((1, LANE), F32).at[0, :DN_HEADS].set(x)

    return pl.pallas_call(
        _dn_kernel,
        grid=(bsz, seq // tc),
        in_specs=[col(cq), col(ck), col(cv), col(cz),
                  pl.BlockSpec((1, tc, LANE), lambda b, t: (b, t, cab // LANE)),
                  _resident((1, DN_DK)), _resident((1, LANE)), _resident((1, LANE))],
        out_specs=pl.BlockSpec((1, tc, dq), lambda b, t: (b, t, 0)),
        out_shape=jax.ShapeDtypeStruct((bsz, seq, dq), BF16),
        scratch_shapes=[pltpu.VMEM((DN_HEADS, DN_DK, DN_DK), F32)],
        compiler_params=_cparams(("parallel", "arbitrary")),
        name="deltanet",
    )(rec, rec, rec, rec, rec, out_norm.reshape(1, DN_DK), lanes(a_log), lanes(dt_bias))


FFN_HALO = 16
FFN_AHEAD = 4


def _ffn_kernel(x_ref, xh_ref, nw_ref, sh_ref, sc_ref, g_ref, wu_ref, cw_ref, wd_ref, out_ref, hb_s, up_s, *, per):
    i = pl.program_id(0)
    tm = x_ref.shape[0]
    nf = wd_ref.shape[0]
    nw, sh, sc = nw_ref[...], sh_ref[0], sc_ref[0]
    x = x_ref[...]
    live = (i % per > 0).astype(F32)
    hb_s[0:FFN_HALO, :] = (_modulate(xh_ref[...], nw, sc, sh) * live).astype(BF16)
    hb_s[FFN_HALO:, :] = _modulate(x, nw, sc, sh).astype(BF16)
    hb = hb_s[...]

    def conv(slot, half, w):
        y = w[FFN_CONV - 1:FFN_CONV, :] * up_s[slot, half, FFN_HALO:, :]
        for s in range(1, FFN_CONV):
            y = y + w[FFN_CONV - 1 - s:FFN_CONV - s, :] * up_s[slot, half, FFN_HALO - s:FFN_HALO - s + tm, :]
        return y

    nslot, cw = up_s.shape[0], up_s.shape[-1]
    cols = lambda half, f: slice(half * nf * cw + f * cw, half * nf * cw + (f + 1) * cw)

    def up(f):
        for half in range(2):
            up_s[f % nslot, half] = jnp.dot(hb, wu_ref[:, cols(half, f)], preferred_element_type=F32)

    acc = jnp.zeros((tm, out_ref.shape[-1]), F32)
    for f in range(min(FFN_AHEAD, nf)):
        up(f)
    for f in range(nf):
        if f + FFN_AHEAD < nf:
            up(f + FFN_AHEAD)
        act = _silu(conv(f % nslot, 0, cw_ref[:, cols(0, f)])) * conv(f % nslot, 1, cw_ref[:, cols(1, f)])
        acc = acc + jnp.dot(act.astype(BF16), wd_ref[f], preferred_element_type=F32)
    out_ref[...] = x + g_ref[0] * acc


def conv_ffn(x2, seq, nw, shift, scale, gate, w_up, conv_w, w_down, tm=512, cw=256):
    n, d = x2.shape
    dff = w_down.shape[0]
    nf = dff // cw
    per = seq // tm
    wu = w_up.astype(BF16)
    cwt = conv_w
    wd = w_down.astype(BF16).reshape(nf, cw, d)
    row = lambda i: (i, 0)
    bat = lambda i: (i // per, 0, 0)
    hpt = tm // FFN_HALO
    return pl.pallas_call(
        functools.partial(_ffn_kernel, per=per),
        grid=(n // tm,),
        in_specs=[pl.BlockSpec((tm, d), row),
                  pl.BlockSpec((FFN_HALO, d), lambda i: (jnp.maximum(i * hpt - 1, 0), 0)),
                  _resident((1, d)), pl.BlockSpec((1, 1, d), bat), pl.BlockSpec((1, 1, d), bat),
                  pl.BlockSpec((1, 1, d), bat),
                  _resident(wu.shape), _resident(cwt.shape), _resident(wd.shape)],
        out_specs=pl.BlockSpec((tm, d), row),
        out_shape=jax.ShapeDtypeStruct((n, d), F32),
        scratch_shapes=[pltpu.VMEM((FFN_HALO + tm, d), BF16),
                        pltpu.VMEM((FFN_AHEAD + 1, 2, FFN_HALO + tm, cw), F32)],
        compiler_params=_cparams(("parallel",)),
        name="ffn",
    )(x2, x2, nw.reshape(1, d), shift, scale, gate, wu, cwt, wd)


def _alibi_slopes(n):
    return jnp.asarray(2.0 ** (-8.0 * np.arange(1, n + 1) / n), dtype=F32)


def _pad_cols(w, mult):
    pad = (-w.shape[1]) % mult
    return jnp.pad(w, ((0, 0), (0, pad)))


def _rec_weight_layout(w):
    dq = DN_HEADS * DN_DK
    nu = S5_WIDTH
    w2 = _pad_cols(jnp.concatenate([w[:, nu:nu + 4 * dq], w[:, :nu], w[:, nu + 4 * dq:]], axis=1), LANE)
    cols = (0, dq, 2 * dq, 3 * dq, 4 * dq + nu)
    return w2, cols, 4 * dq


def kernel(x, c, ada_w, ada_b, norm_mix, norm_ffn, attn_w_in, attn_q_norm_a, attn_k_norm_a, attn_q_norm_b,
           attn_k_norm_b, attn_sinks, attn_w_out, rec_w_in, s5_lambda_re, s5_lambda_im, s5_log_dt, s5_b_re,
           s5_b_im, s5_c_re, s5_c_im, s5_d, s5_glu_w, s5_glu_b, dn_conv, dn_a_log, dn_dt_bias, dn_out_norm,
           rec_w_out, ffn_w_up, ffn_conv, ffn_w_down):
    bsz, seq, d = x.shape
    depth = ada_w.shape[0]
    mod = adaln(c, ada_w, ada_b)
    x2 = x.reshape(bsz * seq, d)
    slopes = _alibi_slopes(N_ATTN_HEADS)
    for layer in range(depth):
        sh1, sc1, g1, sh2, sc2, g2 = [mod[layer, :, j * d:(j + 1) * d].reshape(bsz, 1, d) for j in range(6)]
        i = layer // 2
        if layer % 2 == 0:
            qkv = inproj(x2, seq, norm_mix[layer], sh1, sc1, attn_w_in[i].astype(BF16))
            qkv = qkv.reshape(bsz, seq, -1)
            oa, ob = attention(qkv, slopes, attn_sinks[i], attn_q_norm_a[i], attn_k_norm_a[i],
                               attn_q_norm_b[i], attn_k_norm_b[i])
            x2 = mix_out(x2, seq, g1, oa.reshape(bsz * seq, -1), ob.reshape(bsz * seq, -1),
                         attn_w_out[i].astype(BF16), "attn_out")
        else:
            w_rec, rec_cols, ucol = _rec_weight_layout(rec_w_in[i])
            rec = inproj_conv(x2, seq, norm_mix[layer], sh1, sc1, w_rec.astype(BF16), dn_conv[i])
            rec = rec.reshape(bsz, seq, -1)
            yc = s5_mixer(rec, s5_lambda_re[i], s5_lambda_im[i], s5_log_dt[i], s5_b_re[i], s5_b_im[i],
                          s5_c_re[i], s5_c_im[i], s5_d[i], s5_glu_w[i], s5_glu_b[i], ucol=ucol)
            yd = deltanet_mixer(rec, dn_a_log[i], dn_dt_bias[i], dn_out_norm[i], rec_cols)
            x2 = mix_out(x2, seq, g1, yc.reshape(bsz * seq, -1), yd.reshape(bsz * seq, -1),
                         rec_w_out[i].astype(BF16), "rec_out")
        x2 = conv_ffn(x2, seq, norm_ffn[layer], sh2, sc2, g2, ffn_w_up[layer], ffn_conv[layer], ffn_w_down[layer])
    return x2.reshape(bsz, seq, d)
```

```python
import functools
import math

import numpy as np
import jax
import jax.numpy as jnp
from jax import lax
from jax.experimental import pallas as pl
from jax.experimental.pallas import tpu as pltpu

F32 = jnp.float32
BF16 = jnp.bfloat16
HIGHEST = lax.Precision.HIGHEST
EPS = 1e-6
NEG = -1e30

HEAD_DIM = 64
BLOCK = 128
A_Q_HEADS = 8
A_KV_HEADS = 2
A_WINDOW = 128
B_HEADS = 8
B_BRANCHES = ((128, 1), (512, 4), (2048, 16))
N_ATTN_HEADS = A_Q_HEADS + B_HEADS
S5_GROUP = 16
S5_GROUPS = 16
S5_WIDTH = 256
S5_STATE = 64
DN_HEADS = 6
DN_DK = 128
DN_CONV = 4
DN_CHUNK = 64
FFN_CONV = 3

V7X_VMEM_LIMIT = 60 * 1024 * 1024
LANE = 128


def _cparams(sem, vmem=V7X_VMEM_LIMIT):
    return pltpu.CompilerParams(dimension_semantics=sem, vmem_limit_bytes=vmem)


def _resident(shape):
    n = len(shape)
    return pl.BlockSpec(shape, lambda *_: (0,) * n, pipeline_mode=pl.Buffered(1))


def _sigmoid(x):
    return 1.0 / (1.0 + jnp.exp(-x))


def _silu(x):
    h = 0.5 * x
    return h + h * jnp.tanh(h)


def _modulate(x, nw, scale, shift):
    ms = jnp.mean(x * x, axis=-1, keepdims=True)
    y = x * lax.rsqrt(ms + EPS) * nw
    return y * (1.0 + scale) + shift


def _adaln_kernel(c_ref, w_ref, b_ref, o_ref):
    c = c_ref[...]
    o_ref[0] = jnp.dot(_silu(c), w_ref[0], precision=HIGHEST, preferred_element_type=F32) + b_ref[0]


def adaln(c, ada_w, ada_b, tn=1536):
    depth, d, n = ada_w.shape
    bsz = c.shape[0]
    cp = jnp.zeros((8, d), F32).at[:bsz].set(c)
    out = pl.pallas_call(
        _adaln_kernel,
        grid=(depth, n // tn),
        in_specs=[pl.BlockSpec((8, d), lambda l, j: (0, 0)),
                  pl.BlockSpec((1, d, tn), lambda l, j: (l, 0, j)),
                  pl.BlockSpec((1, 1, tn), lambda l, j: (l, 0, j))],
        out_specs=pl.BlockSpec((1, 8, tn), lambda l, j: (l, 0, j)),
        out_shape=jax.ShapeDtypeStruct((depth, 8, n), F32),
        compiler_params=_cparams(("arbitrary", "arbitrary")),
        name="adaln",
    )(cp, ada_w, ada_b.reshape(depth, 1, n))
    return out[:, :bsz]


def _inproj_kernel(x_ref, nw_ref, sh_ref, sc_ref, w_ref, o_ref):
    h = _modulate(x_ref[...], nw_ref[...], sc_ref[0], sh_ref[0]).astype(BF16)
    o_ref[...] = jnp.dot(h, w_ref[...], preferred_element_type=F32)


def inproj(x2, seq, nw, shift, scale, w, tm=512):
    n, d = x2.shape
    nout = w.shape[1]
    per = seq // tm
    return pl.pallas_call(
        _inproj_kernel,
        grid=(n // tm,),
        in_specs=[pl.BlockSpec((tm, d), lambda i: (i, 0)),
                  _resident((1, d)),
                  pl.BlockSpec((1, 1, d), lambda i: (i // per, 0, 0)),
                  pl.BlockSpec((1, 1, d), lambda i: (i // per, 0, 0)),
                  _resident((d, nout))],
        out_specs=pl.BlockSpec((tm, nout), lambda i: (i, 0)),
        out_shape=jax.ShapeDtypeStruct((n, nout), F32),
        compiler_params=_cparams(("parallel",)),
        name="inproj",
    )(x2, nw.reshape(1, d), shift, scale, w)


CONV_HALO = 16


def _inproj_conv_kernel(x_ref, xh_ref, nw_ref, sh_ref, sc_ref, w_ref, cw_ref, o_ref, hb_s, up_s, *, per, chunk):
    i = pl.program_id(0)
    tm = x_ref.shape[0]
    nout = o_ref.shape[-1]
    ntap, nconv = cw_ref.shape
    nw, sh, sc = nw_ref[...], sh_ref[0], sc_ref[0]
    live = (i % per > 0).astype(F32)
    hb_s[0:CONV_HALO, :] = (_modulate(xh_ref[...], nw, sc, sh) * live).astype(BF16)
    hb_s[CONV_HALO:, :] = _modulate(x_ref[...], nw, sc, sh).astype(BF16)
    hb = hb_s[...]
    bounds = [(c0, min(c0 + chunk, nout)) for c0 in range(0, nout, chunk)]

    def project(n):
        c0, c1 = bounds[n]
        up_s[n % 2, :, 0:c1 - c0] = jnp.dot(hb, w_ref[:, c0:c1], preferred_element_type=F32)

    def finish(n):
        c0, c1 = bounds[n]
        wid = c1 - c0
        if c0 < nconv:
            w = cw_ref[:, c0:c1]
            y = w[ntap - 1:ntap, :] * up_s[n % 2, CONV_HALO:, 0:wid]
            for s in range(1, ntap):
                y = y + w[ntap - 1 - s:ntap - s, :] * up_s[n % 2, CONV_HALO - s:CONV_HALO - s + tm, 0:wid]
            o_ref[:, c0:c1] = _silu(y)
        else:
            o_ref[:, c0:c1] = up_s[n % 2, CONV_HALO:, 0:wid]

    project(0)
    for n in range(len(bounds)):
        if n + 1 < len(bounds):
            project(n + 1)
        finish(n)


def inproj_conv(x2, seq, nw, shift, scale, w, conv_w, tm=512, chunk=768):
    n, d = x2.shape
    nout = w.shape[1]
    per = seq // tm
    hpt = tm // CONV_HALO
    assert conv_w.shape[1] % chunk == 0
    return pl.pallas_call(
        functools.partial(_inproj_conv_kernel, per=per, chunk=chunk),
        grid=(n // tm,),
        in_specs=[pl.BlockSpec((tm, d), lambda i: (i, 0)),
                  pl.BlockSpec((CONV_HALO, d), lambda i: (jnp.maximum(i * hpt - 1, 0), 0)),
                  _resident((1, d)),
                  pl.BlockSpec((1, 1, d), lambda i: (i // per, 0, 0)),
                  pl.BlockSpec((1, 1, d), lambda i: (i // per, 0, 0)),
                  _resident((d, nout)), _resident(conv_w.shape)],
        out_specs=pl.BlockSpec((tm, nout), lambda i: (i, 0)),
        out_shape=jax.ShapeDtypeStruct((n, nout), F32),
        scratch_shapes=[pltpu.VMEM((CONV_HALO + tm, d), BF16), pltpu.VMEM((2, CONV_HALO + tm, chunk), F32)],
        compiler_params=_cparams(("parallel",)),
        name="inproj_conv",
    )(x2, x2, nw.reshape(1, d), shift, scale, w, conv_w)


ATT_SPAN = 2048
ATT_GROUP = 16
ATT_PREP = 512


def _static_or_multiple(x, k):
    return x if isinstance(x, int) else pl.multiple_of(x, k)


def _attn_kernel(slopes_ref, sinks_ref, q_ref, k_ref, v_ref, qw_ref, kw_ref, o_ref, *scratch,
                 branches, head0, with_sinks, dup_kv):
    p = pl.program_id(1)
    t = pl.program_id(2)
    span = q_ref.shape[1]
    nbr = len(branches)
    assert branches[0][0] == 1
    need32 = nbr > 1
    it = iter(scratch)
    qn_s, kn_s = (next(it), next(it)) if need32 else (None, None)
    per = [(next(it), next(it), next(it), next(it)) for _ in branches]
    acc_s, m_s, l_s = (next(it), next(it), next(it)) if nbr > 1 else (None, None, None)

    lane = lax.broadcasted_iota(jnp.int32, (1, LANE), 1)
    low = lane < HEAD_DIM
    li = lax.broadcasted_iota(jnp.int32, (LANE, LANE), 0) // HEAD_DIM
    lj = lax.broadcasted_iota(jnp.int32, (LANE, LANE), 1) // HEAD_DIM
    same_head = (li == lj).astype(BF16)

    for (d, _), (_, ks, vs, _) in zip(branches, per):
        n_d = span // d

        @pl.when(t == 0)
        def _():
            for r in range(d):
                ks[r, 0:BLOCK, :] = jnp.zeros((BLOCK, LANE), BF16)
                for j in range(2):
                    vs[j, r, 0:BLOCK, :] = jnp.zeros((BLOCK, LANE), BF16)

        @pl.when(t > 0)
        def _():
            for r in range(d):
                ks[r, 0:BLOCK, :] = ks[r, n_d:n_d + BLOCK, :]
                for j in range(2):
                    vs[j, r, 0:BLOCK, :] = vs[j, r, n_d:n_d + BLOCK, :]

    @pl.when(t == 0)
    def _():
        qi = lax.broadcasted_iota(jnp.int32, (BLOCK, 2 * BLOCK), 0)
        si = lax.broadcasted_iota(jnp.int32, (BLOCK, 2 * BLOCK), 1)
        dist = BLOCK + qi - si
        for (d, max_dist), (_, _, _, bias) in zip(branches, per):
            valid = (dist >= 0) & (dist <= max_dist)
            valid0 = valid & (si >= BLOCK)
            distf = (d * dist).astype(F32)
            for j in range(2):
                b = -slopes_ref[head0 + 2 * p + j] * distf
                bias[0, j] = jnp.where(valid0, b, NEG)
                bias[1, j] = jnp.where(valid, b, NEG)

    def head_norm(x, w):
        x2 = x * x
        hi = x2.astype(BF16)
        lo = (x2 - hi.astype(F32)).astype(BF16)
        ss = jnp.dot(hi, same_head, preferred_element_type=F32) + jnp.dot(lo, same_head, preferred_element_type=F32)
        return x * lax.rsqrt(ss * (1.0 / HEAD_DIM) + EPS) * w

    def both_halves(x):
        swapped = pltpu.roll(x, HEAD_DIM, 1)
        return jnp.where(lane // HEAD_DIM == p // 2, x, swapped)

    def put_q(qs, rows, x):
        qs[0, rows, :] = jnp.where(low, x, 0.0).astype(BF16)
        qs[1, rows, :] = jnp.where(low, 0.0, x).astype(BF16)

    def put_v(vs, r, rows, x):
        vs[0, r, rows, :] = jnp.where(low, x, 1.0).astype(BF16)
        vs[1, r, rows, :] = jnp.where(low, 1.0, x).astype(BF16)

    qw, kw = qw_ref[...], kw_ref[...]
    qs1, ks1, vs1, _ = per[0]

    def prep(c, _):
        r0 = pl.multiple_of(c * ATT_PREP, ATT_PREP)
        rows = pl.ds(r0, ATT_PREP)
        krows = pl.ds(pl.multiple_of(BLOCK + r0, BLOCK), ATT_PREP)
        qn = head_norm(q_ref[0, rows, :], qw) * (HEAD_DIM ** -0.5)
        xk, xv = k_ref[0, rows, :], v_ref[0, rows, :]
        if dup_kv:
            xk, xv = both_halves(xk), both_halves(xv)
        kn = head_norm(xk, kw)
        if need32:
            qn_s[rows, :] = qn
            kn_s[rows, :] = kn
        put_q(qs1, rows, qn)
        ks1[0, krows, :] = kn.astype(BF16)
        put_v(vs1, 0, krows, xv)
        return 0

    lax.fori_loop(0, span // ATT_PREP, prep, 0)
    for (d, _), (qs, ks, vs, _) in zip(branches[1:], per[1:]):
        n_d = span // d
        for r in range(d):
            sub = pl.ds(r, n_d, stride=d)
            put_q(qs, slice(r * n_d, (r + 1) * n_d), qn_s[sub, :])
            ks[r, BLOCK:BLOCK + n_d, :] = kn_s[sub, :].astype(BF16)
            put_v(vs, r, slice(BLOCK, BLOCK + n_d), v_ref[0, sub, :])

    def process(bi, combos):
        d, _ = branches[bi]
        qs, ks, vs, bias = per[bi]
        n_d = span // d
        qv, kv, vv, bs, out_rows = [], [], [], [], []
        for r, n in combos:
            qrow = _static_or_multiple(r * n_d + n * BLOCK, BLOCK)
            krow = _static_or_multiple(n * BLOCK, BLOCK)
            bidx = jnp.where(jnp.logical_and(t == 0, n == 0), 0, 1)
            for j in range(2):
                qv.append(qs[j, pl.ds(qrow, BLOCK), :])
                kv.append(ks[r, pl.ds(krow, 2 * BLOCK), :])
                vv.append(vs[j, r, pl.ds(krow, 2 * BLOCK), :])
                bs.append(bias[bidx, j])
            out_rows.append(pl.ds(krow, BLOCK) if d == 1 else pl.ds(r + d * BLOCK * n, BLOCK, stride=d))
        s = [lax.dot_general(q, k, (((1,), (1,)), ((), ())), preferred_element_type=F32) + b
             for q, k, b in zip(qv, kv, bs)]
        m = [jnp.max(x, axis=-1, keepdims=True) for x in s]
        pr = [jnp.exp(x - mm).astype(BF16) for x, mm in zip(s, m)]
        acc = [jnp.dot(x, v, preferred_element_type=F32) for x, v in zip(pr, vv)]
        for ci, rows in enumerate(out_rows):
            a0, a1, m0, m1 = acc[2 * ci], acc[2 * ci + 1], m[2 * ci], m[2 * ci + 1]
            num = jnp.where(low, a0, a1)
            den = pltpu.roll(jnp.where(low, a1, a0), HEAD_DIM, 1)
            mx = jnp.where(low, m0, m1)
            if nbr == 1:
                if with_sinks:
                    sk = jnp.where(low, sinks_ref[2 * p], sinks_ref[2 * p + 1])
                    den = den + jnp.exp(sk - mx)
                o_ref[0, rows, :] = (num / den).astype(o_ref.dtype)
            elif bi == 0:
                acc_s[rows, :] = num
                l_s[rows, :] = den
                m_s[rows, :] = mx
            else:
                m_old = m_s[rows, :]
                m_new = jnp.maximum(m_old, mx)
                w_old, w_new = jnp.exp(m_old - m_new), jnp.exp(mx - m_new)
                num = w_old * acc_s[rows, :] + w_new * num
                den = w_old * l_s[rows, :] + w_new * den
                if bi == nbr - 1:
                    acc_s[rows, :] = num / den
                else:
                    acc_s[rows, :] = num
                    l_s[rows, :] = den
                    m_s[rows, :] = m_new

    for bi, (d, _) in enumerate(branches):
        nblk = span // d // BLOCK

        def group(g, _, bi=bi, nblk=nblk):
            if nblk >= ATT_GROUP:
                base = g * ATT_GROUP
                r = base // nblk
                combos = [(r, base % nblk + i) for i in range(ATT_GROUP)]
            else:
                per_grp = ATT_GROUP // nblk
                combos = [(g * per_grp + i // nblk, i % nblk) for i in range(ATT_GROUP)]
            process(bi, combos)
            return 0

        lax.fori_loop(0, span // BLOCK // ATT_GROUP, group, 0)
    if nbr > 1:
        o_ref[0] = acc_s[...].astype(o_ref.dtype)


def _attention_call(qkv, slopes, sinks, qw, kw, *, branches, qblk, kblk, vblk, shared_kv, head0, with_sinks, name):
    bsz, seq, _ = qkv.shape
    span = min(seq, ATT_SPAN)
    npair = 4
    kmap = (lambda b, p, t: (b, t, kblk)) if shared_kv else (lambda b, p, t: (b, t, kblk + p))
    vmap = (lambda b, p, t: (b, t, vblk)) if shared_kv else (lambda b, p, t: (b, t, vblk + p))
    smem = pl.BlockSpec(memory_space=pltpu.SMEM)
    scratch = []
    if len(branches) > 1:
        scratch += [pltpu.VMEM((span, LANE), F32), pltpu.VMEM((span, LANE), F32)]
    for d, _ in branches:
        n_d = span // d
        scratch += [pltpu.VMEM((2, span, LANE), BF16), pltpu.VMEM((d, BLOCK + n_d, LANE), BF16),
                    pltpu.VMEM((2, d, BLOCK + n_d, LANE), BF16), pltpu.VMEM((2, 2, BLOCK, 2 * BLOCK), F32)]
    if len(branches) > 1:
        scratch += [pltpu.VMEM((span, LANE), F32)] * 3
    tile2 = lambda w: jnp.tile(w.reshape(1, HEAD_DIM), (1, 2))
    return pl.pallas_call(
        functools.partial(_attn_kernel, branches=branches, head0=head0, with_sinks=with_sinks, dup_kv=shared_kv),
        grid=(bsz, npair, seq // span),
        in_specs=[smem, smem,
                  pl.BlockSpec((1, span, LANE), lambda b, p, t: (b, t, qblk + p)),
                  pl.BlockSpec((1, span, LANE), kmap),
                  pl.BlockSpec((1, span, LANE), vmap),
                  _resident((1, LANE)), _resident((1, LANE))],
        out_specs=pl.BlockSpec((1, span, LANE), lambda b, p, t: (b, t, p)),
        out_shape=jax.ShapeDtypeStruct((bsz, seq, npair * LANE), BF16),
        scratch_shapes=scratch,
        compiler_params=_cparams(("parallel", "parallel", "arbitrary")),
        name=name,
    )(slopes, sinks, qkv, qkv, qkv, tile2(qw), tile2(kw))


def attention(qkv, slopes, sinks, qw_a, kw_a, qw_b, kw_b):
    oa = _attention_call(qkv, slopes, sinks, qw_a, kw_a, branches=((1, A_WINDOW - 1),), qblk=0, kblk=4, vblk=5,
                         shared_kv=True, head0=0, with_sinks=True, name="attn_a")
    ob = _attention_call(qkv, slopes, sinks, qw_b, kw_b, branches=tuple((d, w // d) for w, d in B_BRANCHES),
                         qblk=6, kblk=10, vblk=14, shared_kv=False, head0=A_Q_HEADS, with_sinks=False, name="attn_b")
    return oa, ob


def _mix_out_kernel(x_ref, g_ref, ya_ref, yb_ref, w_ref, out_ref):
    na = ya_ref.shape[-1]
    y = jnp.dot(ya_ref[...], w_ref[:na, :], preferred_element_type=F32)
    y = y + jnp.dot(yb_ref[...], w_ref[na:, :], preferred_element_type=F32)
    out_ref[...] = x_ref[...] + g_ref[0] * y


def mix_out(x2, seq, gate, ya, yb, w, name, tm=512):
    n, d = x2.shape
    per = seq // tm
    row = lambda i: (i, 0)
    return pl.pallas_call(
        _mix_out_kernel,
        grid=(n // tm,),
        in_specs=[pl.BlockSpec((tm, d), row), pl.BlockSpec((1, 1, d), lambda i: (i // per, 0, 0)),
                  pl.BlockSpec((tm, ya.shape[-1]), row), pl.BlockSpec((tm, yb.shape[-1]), row), _resident(w.shape)],
        out_specs=pl.BlockSpec((tm, d), row),
        out_shape=jax.ShapeDtypeStruct((n, d), F32),
        compiler_params=_cparams(("parallel",)),
        name=name,
    )(x2, gate, ya, yb, w)


def _gelu_tanh(x):
    return 0.5 * x * (1.0 + jnp.tanh(math.sqrt(2.0 / math.pi) * (x + 0.044715 * (x * x * x))))


def _s5_kernel(u_ref, e_ref, pr_ref, pi_ref, cr_ref, ci_ref, d_ref, gw_ref, gb_ref, o_ref, xr_s, xi_s, car_s):
    t = pl.program_id(1)
    tc = u_ref.shape[1]
    ns = pr_ref.shape[1]

    @pl.when(t == 0)
    def _():
        car_s[...] = jnp.zeros_like(car_s)

    u = u_ref[0]
    e = jnp.dot(u.astype(BF16), e_ref[...], preferred_element_type=F32)
    xr, xi = e[:, :ns], e[:, ns:]
    xr = xr.reshape(tc // 8, 8, ns)
    xi = xi.reshape(tc // 8, 8, ns)
    row = lax.broadcasted_iota(jnp.int32, (8, ns), 0)
    for s in (1, 2, 4):
        keep = row >= s
        ar = jnp.where(keep, pr_ref[s - 1:s, :], 0.0)
        ai = jnp.where(keep, pi_ref[s - 1:s, :], 0.0)
        sr, si = pltpu.roll(xr, s, 1), pltpu.roll(xi, s, 1)
        xr, xi = xr + (ar * sr - ai * si), xi + (ar * si + ai * sr)
    xr_s[...] = xr.reshape(tc, ns)
    xi_s[...] = xi.reshape(tc, ns)
    pr, pi = pr_ref[...], pi_ref[...]

    def grp(k, carry):
        cr, ci = carry
        r0 = pl.multiple_of(k * 8, 8)
        nr = xr_s[pl.ds(r0, 8), :] + (pr * cr - pi * ci)
        ni = xi_s[pl.ds(r0, 8), :] + (pr * ci + pi * cr)
        xr_s[pl.ds(r0, 8), :] = nr
        xi_s[pl.ds(r0, 8), :] = ni
        return nr[7:8, :], ni[7:8, :]

    cr, ci = lax.fori_loop(0, tc // 8, grp, (car_s[0:1, :], car_s[1:2, :]))
    car_s[0:1, :] = cr
    car_s[1:2, :] = ci
    y = (jnp.dot(xr_s[...].astype(BF16), cr_ref[...], preferred_element_type=F32)
         - jnp.dot(xi_s[...].astype(BF16), ci_ref[...], preferred_element_type=F32)
         + d_ref[...] * u)
    g = _gelu_tanh(y)
    z = jnp.dot(g.astype(BF16), gw_ref[...], preferred_element_type=F32) + gb_ref[...]
    o_ref[0] = (g * _sigmoid(z)).astype(o_ref.dtype)


def s5_mixer(rec, lam_re, lam_im, log_dt, b_re, b_im, c_re, c_im, d_skip, glu_w, glu_b, ucol=0, tc=1024):
    bsz, seq, _ = rec.shape
    ublk = ucol // S5_WIDTH
    ns = S5_GROUPS * S5_STATE
    dt = jnp.exp(log_dt)[:, None]
    lr, li = lam_re, lam_im
    mag, ang = jnp.exp(lr * dt), li * dt
    ab_re, ab_im = mag * jnp.cos(ang), mag * jnp.sin(ang)
    nr, ni = ab_re - 1.0, ab_im
    den = lr * lr + li * li
    f_re = (nr * lr + ni * li) / den
    f_im = (ni * lr - nr * li) / den
    kk = jnp.arange(1, 9, dtype=F32)[:, None, None]
    pmag = jnp.exp(kk * (lr * dt)[None])
    p_re = (pmag * jnp.cos(kk * ang[None])).reshape(8, ns)
    p_im = (pmag * jnp.sin(kk * ang[None])).reshape(8, ns)
    e_re = f_re[..., None] * b_re - f_im[..., None] * b_im
    e_im = f_re[..., None] * b_im + f_im[..., None] * b_re
    eye = jnp.eye(S5_GROUPS, dtype=F32)

    def bd_in(m):
        return jnp.einsum("gpi,gh->gihp", m, eye).reshape(S5_WIDTH, ns)

    def bd_out(m):
        return jnp.einsum("gip,gh->gphi", m, eye).reshape(ns, S5_WIDTH)

    e_mat = jnp.concatenate([bd_in(e_re), bd_in(e_im)], axis=1).astype(BF16)
    return pl.pallas_call(
        _s5_kernel,
        grid=(bsz, seq // tc),
        in_specs=[pl.BlockSpec((1, tc, S5_WIDTH), lambda b, t: (b, t, ublk)),
                  _resident((S5_WIDTH, 2 * ns)), _resident((8, ns)), _resident((8, ns)),
                  _resident((ns, S5_WIDTH)), _resident((ns, S5_WIDTH)),
                  _resident((1, S5_WIDTH)), _resident((S5_WIDTH, S5_WIDTH)), _resident((1, S5_WIDTH))],
        out_specs=pl.BlockSpec((1, tc, S5_WIDTH), lambda b, t: (b, t, 0)),
        out_shape=jax.ShapeDtypeStruct((bsz, seq, S5_WIDTH), BF16),
        scratch_shapes=[pltpu.VMEM((tc, ns), F32), pltpu.VMEM((tc, ns), F32), pltpu.VMEM((8, ns), F32)],
        compiler_params=_cparams(("parallel", "arbitrary")),
        name="s5",
    )(rec, e_mat, p_re, p_im, bd_out(c_re).astype(BF16), bd_out(c_im).astype(BF16),
      d_skip.reshape(1, S5_WIDTH), glu_w.astype(BF16), glu_b.reshape(1, S5_WIDTH))


def _dot_bf(a, b):
    return jnp.dot(a.astype(BF16), b.astype(BF16), preferred_element_type=F32)


def _dot_nt_bf(a, b):
    return lax.dot_general(a.astype(BF16), b.astype(BF16), (((1,), (1,)), ((), ())), preferred_element_type=F32)


def _dot_tn_bf(a, b):
    return lax.dot_general(a.astype(BF16), b.astype(BF16), (((0,), (0,)), ((), ())), preferred_element_type=F32)


def _split_bf(x):
    hi = x.astype(BF16)
    return hi, (x - hi.astype(F32)).astype(BF16)


def _dot_x3(a, b):
    ah, al = _split_bf(a)
    bh, bl = _split_bf(b)
    return (jnp.dot(ah, bh, preferred_element_type=F32) + jnp.dot(ah, bl, preferred_element_type=F32)
            + jnp.dot(al, bh, preferred_element_type=F32))


def _dot_01(m01, x):
    hi = x.astype(BF16)
    r = x - hi.astype(F32)
    mid = r.astype(BF16)
    lo = (r - mid.astype(F32)).astype(BF16)
    return (jnp.dot(m01, hi, preferred_element_type=F32) + jnp.dot(m01, mid, preferred_element_type=F32)
            + jnp.dot(m01, lo, preferred_element_type=F32))


def _dn_kernel(q_ref, k_ref, v_ref, z_ref, ab_ref, nw_ref, alog_ref, dtb_ref, o_ref, s_s):
    t = pl.program_id(1)
    tc = q_ref.shape[1]
    c = DN_CHUNK

    @pl.when(t == 0)
    def _():
        s_s[...] = jnp.zeros_like(s_s)

    def head_l2(x2, scale):
        out = []
        for j in range(DN_HEADS):
            x = x2[:, j * DN_DK:(j + 1) * DN_DK]
            out.append(x * lax.rsqrt(jnp.sum(x * x, axis=-1, keepdims=True) + EPS) * scale)
        return out

    qn = head_l2(q_ref[0], DN_DK ** -0.5)
    kn = head_l2(k_ref[0], 1.0)
    v2 = v_ref[0]
    ab = ab_ref[0]
    bfull = _sigmoid(ab)
    xa = ab + dtb_ref[...]
    gfull = -jnp.exp(alog_ref[...]) * (jnp.maximum(xa, 0.0) + jnp.log(1.0 + jnp.exp(-jnp.abs(xa))))

    ri = lax.broadcasted_iota(jnp.int32, (c, c), 0)
    ci = lax.broadcasted_iota(jnp.int32, (c, c), 1)
    causal = ri >= ci
    strict = ri > ci
    tril = causal.astype(BF16)
    triu = (ri <= ci).astype(F32)
    ones = jnp.ones((c, c), BF16)

    nck = tc // c
    probs = [(i, j) for i in range(nck) for j in range(DN_HEADS)]
    rows = lambda x, i: x[i * c:(i + 1) * c]
    qc = [rows(qn[j], i) for i, j in probs]
    kc = [rows(kn[j], i) for i, j in probs]
    vc = [rows(v2[:, j * DN_DK:(j + 1) * DN_DK], i) for i, j in probs]
    bc = [rows(bfull[:, DN_HEADS + j:DN_HEADS + j + 1], i) for i, j in probs]
    gb = [jnp.broadcast_to(rows(gfull[:, j:j + 1], i), (c, c)) for i, j in probs]
    cs_l = jnp.concatenate([tril, -ones], axis=1)
    zero = jnp.zeros((c, c), F32)
    cs = [_dot_01(cs_l, jnp.concatenate([jnp.concatenate([x, x], axis=1),
                                         jnp.concatenate([x * triu, zero], axis=1)], axis=0)) for x in gb]
    gdiff = [x[:, :c] for x in cs]
    gi = [x[:, c:c + 1] for x in cs]
    glast = [x[c - 1:c, c:c + 1] for x in cs]
    qkk = [_dot_nt_bf(jnp.concatenate([x, y], axis=0), y) for x, y in zip(qc, kc)]
    gamma = [jnp.where(causal, jnp.exp(jnp.where(causal, x, 0.0)), 0.0) for x in gdiff]
    nm = [jnp.where(strict, b * x[c:] * gm, 0.0) for b, x, gm in zip(bc, qkk, gamma)]
    qk = [x[:c] for x in qkk]
    eye = (ri == ci).astype(F32)
    inv = [eye - x for x in nm]
    pw = [_dot_x3(x, x) for x in nm]
    span = 2
    while span < c:
        inv = [x + _dot_x3(x, y) for x, y in zip(inv, pw)]
        span *= 2
        if span < c:
            pw = [_dot_x3(y, y) for y in pw]
    eg = [jnp.exp(x) for x in gi]
    qk = [x * gm for x, gm in zip(qk, gamma)]
    kd = [k * jnp.exp(gl - g) for k, gl, g in zip(kc, glast, gi)]
    w2 = [jnp.concatenate([_dot_tn_bf(x, t), _dot_bf(y, t)], axis=0).astype(BF16) for x, y, t in zip(kd, qk, inv)]
    ke = [k * e for k, e in zip(kc, eg)]
    ke_hi = [x.astype(BF16) for x in ke]
    ke_lo = [(x - h.astype(F32)).astype(BF16) for x, h in zip(ke, ke_hi)]
    l1 = [jnp.concatenate([h, lo, (q * e).astype(BF16)], axis=0) for h, lo, q, e in zip(ke_hi, ke_lo, qc, eg)]
    decay = [jnp.exp(x) for x in glast]

    nw = nw_ref[...]
    states = [s_s[j] for j in range(DN_HEADS)]
    for i in range(nck):
        sl = slice(i * c, (i + 1) * c)
        ns = [i * DN_HEADS + j for j in range(DN_HEADS)]
        s_hi = [s.astype(BF16) for s in states]
        s_lo = [(s - h.astype(F32)).astype(BF16) for s, h in zip(states, s_hi)]
        p1 = [jnp.dot(l1[n], h, preferred_element_type=F32) for n, h in zip(ns, s_hi)]
        p2 = [jnp.dot(ke_hi[n], lo, preferred_element_type=F32) for n, lo in zip(ns, s_lo)]
        res = [bc[n] * (vc[n] - (a[:c] + a[c:2 * c] + b)) for n, a, b in zip(ns, p1, p2)]
        p3 = [jnp.dot(w2[n], r.astype(BF16), preferred_element_type=F32) for n, r in zip(ns, res)]
        outs = []
        for j in range(DN_HEADS):
            o = p1[j][2 * c:] + p3[j][DN_DK:]
            states[j] = states[j] * decay[ns[j]] + p3[j][:DN_DK]
            outs.append(o * lax.rsqrt(jnp.mean(o * o, axis=-1, keepdims=True) + EPS) * nw)
        o_ref[0, sl, :] = (jnp.concatenate(outs, axis=-1) * _silu(z_ref[0, sl, :])).astype(o_ref.dtype)
    for j in range(DN_HEADS):
        s_s[j] = states[j]


def deltanet_mixer(rec, a_log, dt_bias, out_norm, cols, tc=256):
    bsz, seq, _ = rec.shape
    cq, ck, cv, cz, cab = cols
    dq = DN_HEADS * DN_DK

    def col(c0):
        return pl.BlockSpec((1, tc, dq), lambda b, t: (b, t, c0 // dq))

    def lanes(x):
        return jnp.zeros((1, LANE), F32).at[0, :DN_HEADS].set(x)

    return pl.pallas_call(
        _dn_kernel,
        grid=(bsz, seq // tc),
        in_specs=[col(cq), col(ck), col(cv), col(cz),
                  pl.BlockSpec((1, tc, LANE), lambda b, t: (b, t, cab // LANE)),
                  _resident((1, DN_DK)), _resident((1, LANE)), _resident((1, LANE))],
        out_specs=pl.BlockSpec((1, tc, dq), lambda b, t: (b, t, 0)),
        out_shape=jax.ShapeDtypeStruct((bsz, seq, dq), BF16),
        scratch_shapes=[pltpu.VMEM((DN_HEADS, DN_DK, DN_DK), F32)],
        compiler_params=_cparams(("parallel", "arbitrary")),
        name="deltanet",
    )(rec, rec, rec, rec, rec, out_norm.reshape(1, DN_DK), lanes(a_log), lanes(dt_bias))


FFN_HALO = 16
FFN_AHEAD = 4


def _ffn_kernel(x_ref, xh_ref, nw_ref, sh_ref, sc_ref, g_ref, wu_ref, cw_ref, wd_ref, out_ref, hb_s, up_s, *, per):
    i = pl.program_id(0)
    tm = x_ref.shape[0]
    nf = wd_ref.shape[0]
    nw, sh, sc = nw_ref[...], sh_ref[0], sc_ref[0]
    x = x_ref[...]
    live = (i % per > 0).astype(F32)
    hb_s[0:FFN_HALO, :] = (_modulate(xh_ref[...], nw, sc, sh) * live).astype(BF16)
    hb_s[FFN_HALO:, :] = _modulate(x, nw, sc, sh).astype(BF16)
    hb = hb_s[...]

    def conv(slot, half, w):
        y = w[FFN_CONV - 1:FFN_CONV, :] * up_s[slot, half, FFN_HALO:, :]
        for s in range(1, FFN_CONV):
            y = y + w[FFN_CONV - 1 - s:FFN_CONV - s, :] * up_s[slot, half, FFN_HALO - s:FFN_HALO - s + tm, :]
        return y

    nslot, cw = up_s.shape[0], up_s.shape[-1]
    cols = lambda half, f: slice(half * nf * cw + f * cw, half * nf * cw + (f + 1) * cw)

    def up(f):
        for half in range(2):
            up_s[f % nslot, half] = jnp.dot(hb, wu_ref[:, cols(half, f)], preferred_element_type=F32)

    acc = jnp.zeros((tm, out_ref.shape[-1]), F32)
    for f in range(min(FFN_AHEAD, nf)):
        up(f)
    for f in range(nf):
        if f + FFN_AHEAD < nf:
            up(f + FFN_AHEAD)
        act = _silu(conv(f % nslot, 0, cw_ref[:, cols(0, f)])) * conv(f % nslot, 1, cw_ref[:, cols(1, f)])
        acc = acc + jnp.dot(act.astype(BF16), wd_ref[f], preferred_element_type=F32)
    out_ref[...] = x + g_ref[0] * acc


def conv_ffn(x2, seq, nw, shift, scale, gate, w_up, conv_w, w_down, tm=512, cw=256):
    n, d = x2.shape
    dff = w_down.shape[0]
    nf = dff // cw
    per = seq // tm
    wu = w_up.astype(BF16)
    cwt = conv_w
    wd = w_down.astype(BF16).reshape(nf, cw, d)
    row = lambda i: (i, 0)
    bat = lambda i: (i // per, 0, 0)
    hpt = tm // FFN_HALO
    return pl.pallas_call(
        functools.partial(_ffn_kernel, per=per),
        grid=(n // tm,),
        in_specs=[pl.BlockSpec((tm, d), row),
                  pl.BlockSpec((FFN_HALO, d), lambda i: (jnp.maximum(i * hpt - 1, 0), 0)),
                  _resident((1, d)), pl.BlockSpec((1, 1, d), bat), pl.BlockSpec((1, 1, d), bat),
                  pl.BlockSpec((1, 1, d), bat),
                  _resident(wu.shape), _resident(cwt.shape), _resident(wd.shape)],
        out_specs=pl.BlockSpec((tm, d), row),
        out_shape=jax.ShapeDtypeStruct((n, d), F32),
        scratch_shapes=[pltpu.VMEM((FFN_HALO + tm, d), BF16),
                        pltpu.VMEM((FFN_AHEAD + 1, 2, FFN_HALO + tm, cw), F32)],
        compiler_params=_cparams(("parallel",)),
        name="ffn",
    )(x2, x2, nw.reshape(1, d), shift, scale, gate, wu, cwt, wd)


def _alibi_slopes(n):
    return jnp.asarray(2.0 ** (-8.0 * np.arange(1, n + 1) / n), dtype=F32)


def _pad_cols(w, mult):
    pad = (-w.shape[1]) % mult
    return jnp.pad(w, ((0, 0), (0, pad)))


def _rec_weight_layout(w):
    dq = DN_HEADS * DN_DK
    nu = S5_WIDTH
    w2 = _pad_cols(jnp.concatenate([w[:, nu:nu + 4 * dq], w[:, :nu], w[:, nu + 4 * dq:]], axis=1), LANE)
    cols = (0, dq, 2 * dq, 3 * dq, 4 * dq + nu)
    return w2, cols, 4 * dq


def kernel(x, c, ada_w, ada_b, norm_mix, norm_ffn, attn_w_in, attn_q_norm_a, attn_k_norm_a, attn_q_norm_b,
           attn_k_norm_b, attn_sinks, attn_w_out, rec_w_in, s5_lambda_re, s5_lambda_im, s5_log_dt, s5_b_re,
           s5_b_im, s5_c_re, s5_c_im, s5_d, s5_glu_w, s5_glu_b, dn_conv, dn_a_log, dn_dt_bias, dn_out_norm,
           rec_w_out, ffn_w_up, ffn_conv, ffn_w_down):
    bsz, seq, d = x.shape
    depth = ada_w.shape[0]
    mod = adaln(c, ada_w, ada_b)
    x2 = x.reshape(bsz * seq, d)
    slopes = _alibi_slopes(N_ATTN_HEADS)
    for layer in range(depth):
        sh1, sc1, g1, sh2, sc2, g2 = [mod[layer, :, j * d:(j + 1) * d].reshape(bsz, 1, d) for j in range(6)]
        i = layer // 2
        if layer % 2 == 0:
            qkv = inproj(x2, seq, norm_mix[layer], sh1, sc1, attn_w_in[i].astype(BF16))
            qkv = qkv.reshape(bsz, seq, -1)
            oa, ob = attention(qkv, slopes, attn_sinks[i], attn_q_norm_a[i], attn_k_norm_a[i],
                               attn_q_norm_b[i], attn_k_norm_b[i])
            x2 = mix_out(x2, seq, g1, oa.reshape(bsz * seq, -1), ob.reshape(bsz * seq, -1),
                         attn_w_out[i].astype(BF16), "attn_out")
        else:
            w_rec, rec_cols, ucol = _rec_weight_layout(rec_w_in[i])
            rec = inproj_conv(x2, seq, norm_mix[layer], sh1, sc1, w_rec.astype(BF16), dn_conv[i])
            rec = rec.reshape(bsz, seq, -1)
            yc = s5_mixer(rec, s5_lambda_re[i], s5_lambda_im[i], s5_log_dt[i], s5_b_re[i], s5_b_im[i],
                          s5_c_re[i], s5_c_im[i], s5_d[i], s5_glu_w[i], s5_glu_b[i], ucol=ucol)
            yd = deltanet_mixer(rec, dn_a_log[i], dn_dt_bias[i], dn_out_norm[i], rec_cols)
            x2 = mix_out(x2, seq, g1, yc.reshape(bsz * seq, -1), yd.reshape(bsz * seq, -1),
                         rec_w_out[i].astype(BF16), "rec_out")
        x2 = conv_ffn(x2, seq, norm_ffn[layer], sh2, sc2, g2, ffn_w_up[layer], ffn_conv[layer], ffn_w_down[layer])
    return x2.reshape(bsz, seq, d)
```
